```python
import math
import jax, jax.numpy as jnp
from jax import lax
import numpy as np

D_MODEL = 1024
BATCH = 16
SEQ = 2048
DEPTH = 2

LRU_WIDTH = 512
LRU_HEADS = 8
LRU_HEAD_DIM = LRU_WIDTH // LRU_HEADS
LRU_CONV = 4
LRU_C = 8.0
MOBA_HEADS = 8
MOBA_HEAD_DIM = 64
MOBA_WIDTH = MOBA_HEADS * MOBA_HEAD_DIM
MOBA_BLOCK = 256
MOBA_TOPK = 3
MOBA_Q_CHUNK = 16
MEM_LEN = 256
XATTN_HEADS = 4
XATTN_HEAD_DIM = 128
XATTN_WIDTH = XATTN_HEADS * XATTN_HEAD_DIM
N_BRANCH = 3
IN_COLS = 2 * LRU_WIDTH + 3 * MOBA_WIDTH + XATTN_WIDTH + N_BRANCH * D_MODEL
D_FF = 3 * D_MODEL
FFN_CONV = 3
NORM_EPS = 1e-6

kernel_name = "hybrid_rglru_moba_xattn_convffn"


def rms_norm(x, g):
    x32 = x.astype(jnp.float32)
    y = x32 * lax.rsqrt(jnp.mean(x32 * x32, axis=-1, keepdims=True) + NORM_EPS)
    return (y * g.astype(jnp.float32)).astype(x.dtype)


def causal_dwconv(x, w, b):
    k, c = w.shape
    y = lax.conv_general_dilated(x, w[:, None, :].astype(x.dtype), (1,), [(k - 1, 0)],
                                 dimension_numbers=("NWC", "WIO", "NWC"),
                                 feature_group_count=c)
    return y + b.astype(x.dtype)


def rg_lru(x, w_a, b_a, w_x, b_x, lam):
    bsz, s, w = x.shape
    xh = x.reshape(bsz, s, LRU_HEADS, LRU_HEAD_DIM)
    r = jax.nn.sigmoid((jnp.einsum("bshi,hij->bshj", xh, w_a) + b_a).reshape(bsz, s, w).astype(jnp.float32))
    i = jax.nn.sigmoid((jnp.einsum("bshi,hij->bshj", xh, w_x) + b_x).reshape(bsz, s, w).astype(jnp.float32))
    log_a = -LRU_C * r * jax.nn.softplus(-lam.astype(jnp.float32))
    a = jnp.exp(log_a)
    mult = jnp.sqrt(-jnp.expm1(2.0 * log_a))
    u = mult * i * x.astype(jnp.float32)

    def combine(lhs, rhs):
        a1, b1 = lhs
        a2, b2 = rhs
        return a1 * a2, a2 * b1 + b2

    _, h = lax.associative_scan(combine, (a, u), axis=1)
    return h.astype(x.dtype)


def alibi_slopes(n_heads):
    return jnp.exp2(-8.0 * jnp.arange(1, n_heads + 1, dtype=jnp.float32) / n_heads)


def moba_attention(q, k, v):
    bsz, nh, s, dh = q.shape
    nb = -(-s // MOBA_BLOCK)
    sp = nb * MOBA_BLOCK
    pad = ((0, 0), (0, 0), (0, sp - s), (0, 0))
    q, k, v = jnp.pad(q, pad), jnp.pad(k, pad), jnp.pad(v, pad)
    kb = k.reshape(bsz, nh, nb, MOBA_BLOCK, dh)
    vb = v.reshape(bsz, nh, nb, MOBA_BLOCK, dh)
    kmean = jnp.mean(kb.astype(jnp.float32), axis=3)
    n_sel = min(MOBA_TOPK, nb - 1)
    slopes = alibi_slopes(nh)[None, :, None]
    scale = dh ** -0.5
    nq = sp // MOBA_Q_CHUNK
    qc = q.reshape(bsz, nh, nq, MOBA_Q_CHUNK, dh).transpose(2, 0, 1, 3, 4)
    bi = jnp.arange(bsz)[:, None, None, None]
    hi = jnp.arange(nh)[None, :, None, None]
    blk_pos = jnp.arange(MOBA_BLOCK)

    def attend_chunk(args):
        qi, c = args
        t = c * MOBA_Q_CHUNK + jnp.arange(MOBA_Q_CHUNK)
        own = (c * MOBA_Q_CHUNK) // MOBA_BLOCK
        k_own = lax.dynamic_index_in_dim(kb, own, axis=2, keepdims=False)
        v_own = lax.dynamic_index_in_dim(vb, own, axis=2, keepdims=False)
        dist_own = (t[:, None] - (own * MOBA_BLOCK + blk_pos)[None, :]).astype(jnp.float32)
        s_own = (jnp.einsum("bhqd,bhkd->bhqk", qi, k_own, preferred_element_type=jnp.float32) * scale
                 - slopes[..., None] * dist_own)
        s_own = jnp.where(dist_own >= 0, s_own, -jnp.inf)
        if n_sel == 0:
            p = jax.nn.softmax(s_own, axis=-1).astype(v.dtype)
            return jnp.einsum("bhqk,bhkd->bhqd", p, v_own)
        gate = jnp.einsum("bhqd,bhnd->bhqn", qi.astype(jnp.float32), kmean)
        gate = jnp.where(jnp.arange(nb) < own, gate, -jnp.inf)
        _, sel = lax.top_k(gate, n_sel)
        valid = sel < own
        ks = kb[bi, hi, sel]
        vs = vb[bi, hi, sel]
        kpos_sel = sel[..., None] * MOBA_BLOCK + blk_pos
        dist_sel = (t[None, None, :, None, None] - kpos_sel).astype(jnp.float32)
        s_sel = (jnp.einsum("bhqd,bhqnkd->bhqnk", qi, ks, preferred_element_type=jnp.float32) * scale
                 - slopes[..., None, None] * dist_sel)
        s_sel = jnp.where(valid[..., None], s_sel, -jnp.inf)
        nsk = n_sel * MOBA_BLOCK
        scores = jnp.concatenate([s_sel.reshape(bsz, nh, MOBA_Q_CHUNK, nsk), s_own], axis=-1)
        p = jax.nn.softmax(scores, axis=-1).astype(v.dtype)
        p_sel = p[..., :nsk].reshape(bsz, nh, MOBA_Q_CHUNK, n_sel, MOBA_BLOCK)
        p_own = p[..., nsk:]
        return (jnp.einsum("bhqnk,bhqnkd->bhqd", p_sel, vs)
                + jnp.einsum("bhqk,bhkd->bhqd", p_own, v_own))

    out = lax.map(attend_chunk, (qc, jnp.arange(nq)))
    out = out.transpose(1, 2, 0, 3, 4).reshape(bsz, nh, sp, dh)
    return out[:, :, :s]


def memory_cross_attention(q, mem_k, mem_v):
    scores = jnp.einsum("bshd,bmhd->bhsm", q, mem_k, preferred_element_type=jnp.float32) * (XATTN_HEAD_DIM ** -0.5)
    p = jax.nn.softmax(scores, axis=-1).astype(mem_v.dtype)
    o = jnp.einsum("bhsm,bmhd->bshd", p, mem_v)
    return o.reshape(q.shape[0], q.shape[1], XATTN_WIDTH)


def setup_inputs(seed: int = 0) -> dict:
    key = jax.random.key(seed)
    ks = jax.random.split(key, 32)
    f32 = jnp.float32

    def nrm(k, shape, scale):
        return jax.random.normal(k, shape, f32) * scale

    def gain(k, shape):
        return 1.0 + 0.02 * jax.random.normal(k, shape, f32)

    u = jax.random.uniform(ks[8], (DEPTH, LRU_WIDTH), f32, 0.9, 0.999)
    sa = u ** (1.0 / LRU_C)
    lru_lambda = jnp.log(sa) - jnp.log1p(-sa)
    return {
        "x": nrm(ks[0], (BATCH, SEQ, D_MODEL), 1.0),
        "mem": nrm(ks[1], (BATCH, MEM_LEN, D_MODEL), 1.0),
        "mix_norm_gain": gain(ks[2], (DEPTH, D_MODEL)),
        "w_in": nrm(ks[3], (DEPTH, D_MODEL, IN_COLS), D_MODEL ** -0.5),
        "lru_conv_w": nrm(ks[4], (DEPTH, LRU_CONV, LRU_WIDTH), LRU_CONV ** -0.5),
        "lru_conv_b": nrm(ks[5], (DEPTH, LRU_WIDTH), 0.02),
        "lru_w_a": nrm(ks[6], (DEPTH, LRU_HEADS, LRU_HEAD_DIM, LRU_HEAD_DIM), LRU_HEAD_DIM ** -0.5),
        "lru_b_a": nrm(ks[7], (DEPTH, LRU_HEADS, LRU_HEAD_DIM), 0.02),
        "lru_w_x": nrm(ks[9], (DEPTH, LRU_HEADS, LRU_HEAD_DIM, LRU_HEAD_DIM), LRU_HEAD_DIM ** -0.5),
        "lru_b_x": nrm(ks[10], (DEPTH, LRU_HEADS, LRU_HEAD_DIM), 0.02),
        "lru_lambda": lru_lambda,
        "mem_norm_gain": gain(ks[11], (DEPTH, D_MODEL)),
        "w_mem_kv": nrm(ks[12], (DEPTH, D_MODEL, 2 * XATTN_WIDTH), D_MODEL ** -0.5),
        "w_branch": nrm(ks[13], (DEPTH, N_BRANCH, LRU_WIDTH, D_MODEL), LRU_WIDTH ** -0.5),
        "w_out": nrm(ks[14], (DEPTH, D_MODEL, D_MODEL), D_MODEL ** -0.5),
        "ffn_norm_gain": gain(ks[15], (DEPTH, D_MODEL)),
        "w_ffn_gate": nrm(ks[16], (DEPTH, D_MODEL, D_FF), D_MODEL ** -0.5),
        "w_ffn_up": nrm(ks[17], (DEPTH, D_MODEL, D_FF), D_MODEL ** -0.5),
        "ffn_conv_w": nrm(ks[18], (DEPTH, FFN_CONV, D_FF), FFN_CONV ** -0.5),
        "ffn_conv_b": nrm(ks[19], (DEPTH, D_FF), 0.02),
        "w_ffn_down": nrm(ks[20], (DEPTH, D_FF, D_MODEL), D_FF ** -0.5),
        "final_norm_gain": gain(ks[21], (D_MODEL,)),
    }


def reference(x, mem, mix_norm_gain, w_in, lru_conv_w, lru_conv_b, lru_w_a, lru_b_a, lru_w_x, lru_b_x,
              lru_lambda, mem_norm_gain, w_mem_kv, w_branch, w_out, ffn_norm_gain, w_ffn_gate, w_ffn_up,
              ffn_conv_w, ffn_conv_b, w_ffn_down, final_norm_gain):
    bsz, s, d = x.shape
    m = mem.shape[1]
    splits = list(np.cumsum([LRU_WIDTH, LRU_WIDTH, MOBA_WIDTH, MOBA_WIDTH, MOBA_WIDTH, XATTN_WIDTH]))
    for l in range(DEPTH):
        h = rms_norm(x, mix_norm_gain[l])
        proj = h @ w_in[l]
        xa, ga, qm, km, vm, qx, g_logits = jnp.split(proj, splits, axis=-1)
        xa = causal_dwconv(xa, lru_conv_w[l], lru_conv_b[l])
        y_a = rg_lru(xa, lru_w_a[l], lru_b_a[l], lru_w_x[l], lru_b_x[l], lru_lambda[l]) * jax.nn.gelu(ga)
        to_heads = lambda t_: t_.reshape(bsz, s, MOBA_HEADS, MOBA_HEAD_DIM).transpose(0, 2, 1, 3)
        y_b = moba_attention(to_heads(qm), to_heads(km), to_heads(vm))
        y_b = y_b.transpose(0, 2, 1, 3).reshape(bsz, s, MOBA_WIDTH)
        mkv = rms_norm(mem, mem_norm_gain[l]) @ w_mem_kv[l]
        mk, mv = jnp.split(mkv.reshape(bsz, m, 2, XATTN_HEADS, XATTN_HEAD_DIM), 2, axis=2)
        y_c = memory_cross_attention(qx.reshape(bsz, s, XATTN_HEADS, XATTN_HEAD_DIM), mk[:, :, 0], mv[:, :, 0])
        ys = jnp.stack([y_a, y_b, y_c], axis=2)
        branch_out = jnp.einsum("bsnw,nwd->bsnd", ys, w_branch[l])
        gates = jax.nn.sigmoid(g_logits.astype(jnp.float32)).astype(x.dtype).reshape(bsz, s, N_BRANCH, d)
        merged = jnp.sum(gates * branch_out, axis=2)
        x = x + merged @ w_out[l]
        h = rms_norm(x, ffn_norm_gain[l])
        g = causal_dwconv(h @ w_ffn_gate[l], ffn_conv_w[l], ffn_conv_b[l])
        x = x + (jax.nn.gelu(g) * (h @ w_ffn_up[l])) @ w_ffn_down[l]
    return rms_norm(x, final_norm_gain)
```

```python
import functools

import jax
import jax.numpy as jnp
from jax import lax
from jax.experimental import pallas as pl
from jax.experimental.pallas import tpu as pltpu

F32 = jnp.float32
BF16 = jnp.bfloat16

D_MODEL = 1024
LRU_WIDTH = 512
LRU_HEADS = 8
LRU_HEAD_DIM = LRU_WIDTH // LRU_HEADS
LRU_CONV = 4
LRU_C = 8.0
MOBA_HEADS = 8
MOBA_HEAD_DIM = 64
MOBA_WIDTH = MOBA_HEADS * MOBA_HEAD_DIM
MOBA_BLOCK = 256
MOBA_TOPK = 3
XATTN_HEADS = 4
XATTN_HEAD_DIM = 128
XATTN_WIDTH = XATTN_HEADS * XATTN_HEAD_DIM
N_BRANCH = 3
D_FF = 3 * D_MODEL
FFN_CONV = 3
NORM_EPS = 1e-6

V7X_SUBLANES = 8
V7X_LANES = 128
VMEM_LIMIT_BYTES = 56 * 1024 * 1024

PROJ_COLS = 2 * LRU_WIDTH + 3 * MOBA_WIDTH + XATTN_WIDTH
NEG_INF = float("-inf")


def _params(*semantics):
    return pltpu.CompilerParams(dimension_semantics=semantics, vmem_limit_bytes=VMEM_LIMIT_BYTES)


def _rms(x, g):
    return x * lax.rsqrt(jnp.mean(x * x, axis=-1, keepdims=True) + NORM_EPS) * g


def _nt_dot(a, b):
    return lax.dot_general(a, b, (((1,), (1,)), ((), ())), preferred_element_type=F32)


def _memkv_kernel(mem_ref, g_ref, w_ref, k_ref, vt_ref):
    h = _rms(mem_ref[0], g_ref[...]).astype(BF16)
    k_ref[0] = jnp.dot(h, w_ref[:, :XATTN_WIDTH], preferred_element_type=F32).astype(BF16)
    v = jnp.dot(h, w_ref[:, XATTN_WIDTH:], preferred_element_type=F32)
    vt_ref[0] = v.T.astype(BF16)


def _memkv(mem, gain, w):
    bsz, m, d = mem.shape
    return pl.pallas_call(
        _memkv_kernel,
        grid=(bsz,),
        in_specs=[
            pl.BlockSpec((1, m, d), lambda b: (b, 0, 0)),
            pl.BlockSpec((1, d), lambda b: (0, 0)),
            pl.BlockSpec((d, 2 * XATTN_WIDTH), lambda b: (0, 0)),
        ],
        out_specs=[
            pl.BlockSpec((1, m, XATTN_WIDTH), lambda b: (b, 0, 0)),
            pl.BlockSpec((1, XATTN_WIDTH, m), lambda b: (b, 0, 0)),
        ],
        out_shape=[
            jax.ShapeDtypeStruct((bsz, m, XATTN_WIDTH), BF16),
            jax.ShapeDtypeStruct((bsz, XATTN_WIDTH, m), BF16),
        ],
        name="memkv",
        compiler_params=_params("arbitrary"),
    )(mem, gain, w)


INPROJ_TM = 512
INPROJ_CHUNK = 512


def _inproj_kernel(x_ref, g_ref, w_ref, xg_ref, qkv_ref):
    h = _rms(x_ref[...], g_ref[...]).astype(BF16)
    n_f32 = 2 * LRU_WIDTH // INPROJ_CHUNK
    for c in range(PROJ_COLS // INPROJ_CHUNK):
        lo = c * INPROJ_CHUNK
        y = jnp.dot(h, w_ref[:, lo:lo + INPROJ_CHUNK], preferred_element_type=F32)
        if c < n_f32:
            xg_ref[:, lo:lo + INPROJ_CHUNK] = y
        else:
            if c == n_f32:
                y = y * (MOBA_HEAD_DIM ** -0.5)
            o = lo - 2 * LRU_WIDTH
            qkv_ref[:, o:o + INPROJ_CHUNK] = y.astype(BF16)


def _inproj(x2, gain, w):
    t, d = x2.shape
    n_bf = PROJ_COLS - 2 * LRU_WIDTH
    return pl.pallas_call(
        _inproj_kernel,
        grid=(t // INPROJ_TM,),
        in_specs=[
            pl.BlockSpec((INPROJ_TM, d), lambda i: (i, 0)),
            pl.BlockSpec((1, d), lambda i: (0, 0)),
            pl.BlockSpec((d, PROJ_COLS), lambda i: (0, 0)),
        ],
        out_specs=[
            pl.BlockSpec((INPROJ_TM, 2 * LRU_WIDTH), lambda i: (i, 0)),
            pl.BlockSpec((INPROJ_TM, n_bf), lambda i: (i, 0)),
        ],
        out_shape=[
            jax.ShapeDtypeStruct((t, 2 * LRU_WIDTH), F32),
            jax.ShapeDtypeStruct((t, n_bf), BF16),
        ],
        name="inproj",
        compiler_params=_params("arbitrary"),
    )(x2, gain, w)


LRU_TM = 512
LRU_UNROLL = 4


def _lru_kernel(xg_ref, cw_ref, cb_ref, wg_ref, bg_ref, lam_ref, y_ref,
                xs_ref, a_ref, u_ref, h_ref, carry_ref):
    tm = LRU_TM
    w = LRU_WIDTH
    first = pl.program_id(1) == 0

    @pl.when(first)
    def _():
        xs_ref[0:V7X_SUBLANES, :] = jnp.zeros((V7X_SUBLANES, w), F32)
        carry_ref[...] = jnp.zeros((V7X_SUBLANES, w), F32)

    @pl.when(jnp.logical_not(first))
    def _():
        xs_ref[0:V7X_SUBLANES, :] = xs_ref[tm:tm + V7X_SUBLANES, :]

    xs_ref[V7X_SUBLANES:V7X_SUBLANES + tm, :] = xg_ref[:, :w]

    xc = cb_ref[...] + cw_ref[LRU_CONV - 1:LRU_CONV, :] * xs_ref[V7X_SUBLANES:V7X_SUBLANES + tm, :]
    for k in range(LRU_CONV - 1):
        off = V7X_SUBLANES - (LRU_CONV - 1) + k
        xc = xc + cw_ref[k:k + 1, :] * xs_ref[off:off + tm, :]

    gates = jnp.dot(xc.astype(BF16), wg_ref[...], preferred_element_type=F32) + bg_ref[...]
    r = jax.nn.sigmoid(gates[:, :w])
    i = jax.nn.sigmoid(gates[:, w:])
    z = -lam_ref[...]
    softplus = jnp.maximum(z, 0.0) + jnp.log1p(jnp.exp(-jnp.abs(z)))
    log_a = (-LRU_C) * r * softplus
    a = jnp.exp(log_a)
    th = jnp.tanh(log_a)
    mult = jnp.sqrt(-2.0 * th / (1.0 - th))
    a_ref[...] = a
    u_ref[...] = mult * i * xc

    sub = lax.broadcasted_iota(jnp.int32, (V7X_SUBLANES, w), 0)

    def body(g, carry):
        r0 = pl.multiple_of(g * V7X_SUBLANES, V7X_SUBLANES)
        av = a_ref[pl.ds(r0, V7X_SUBLANES), :]
        uv = u_ref[pl.ds(r0, V7X_SUBLANES), :]
        for d in (1, 2, 4):
            keep = sub >= d
            a_s = jnp.where(keep, pltpu.roll(av, d, 0), 1.0)
            u_s = jnp.where(keep, pltpu.roll(uv, d, 0), 0.0)
            uv = uv + av * u_s
            av = av * a_s
        hv = uv + av * carry
        h_ref[pl.ds(r0, V7X_SUBLANES), :] = hv
        return jnp.broadcast_to(hv[V7X_SUBLANES - 1:V7X_SUBLANES, :], (V7X_SUBLANES, w))

    carry_ref[...] = lax.fori_loop(0, tm // V7X_SUBLANES, body, carry_ref[...], unroll=LRU_UNROLL)
    y_ref[...] = (h_ref[...] * jax.nn.gelu(xg_ref[:, w:])).astype(BF16)


def _lru(xg, conv_w, conv_b, w_gates, b_gates, lam, bsz, s):
    t = xg.shape[0]
    w = LRU_WIDTH
    nt = s // LRU_TM
    return pl.pallas_call(
        _lru_kernel,
        grid=(bsz, nt),
        in_specs=[
            pl.BlockSpec((LRU_TM, 2 * w), lambda b, j: (b * nt + j, 0)),
            pl.BlockSpec((LRU_CONV, w), lambda b, j: (0, 0)),
            pl.BlockSpec((1, w), lambda b, j: (0, 0)),
            pl.BlockSpec((w, 2 * w), lambda b, j: (0, 0)),
            pl.BlockSpec((1, 2 * w), lambda b, j: (0, 0)),
            pl.BlockSpec((1, w), lambda b, j: (0, 0)),
        ],
        out_specs=pl.BlockSpec((LRU_TM, w), lambda b, j: (b * nt + j, 0)),
        out_shape=jax.ShapeDtypeStruct((t, w), BF16),
        scratch_shapes=[
            pltpu.VMEM((LRU_TM + 2 * V7X_SUBLANES, w), F32),
            pltpu.VMEM((LRU_TM, w), F32),
            pltpu.VMEM((LRU_TM, w), F32),
            pltpu.VMEM((LRU_TM, w), F32),
            pltpu.VMEM((V7X_SUBLANES, w), F32),
        ],
        name="lru",
        compiler_params=_params("arbitrary", "arbitrary"),
    )(xg, conv_w, conv_b, w_gates, b_gates, lam)


MOBA_PAIR = V7X_LANES // MOBA_HEAD_DIM


def _moba_kernel(slopes_ref, q_ref, k_ref, v_ref, o_ref, vt_ref, bias_ref, *, seq):
    nb = seq // MOBA_BLOCK
    blk = MOBA_BLOCK
    hp = pl.program_id(1)
    lane = lax.broadcasted_iota(jnp.int32, (1, V7X_LANES), 1)

    vt_ref[...] = v_ref[...].astype(F32).T.astype(BF16)

    rr = lax.broadcasted_iota(jnp.int32, (seq, blk), 0)
    cc = lax.broadcasted_iota(jnp.int32, (seq, blk), 1)
    dist = (nb - 1) * blk + cc - rr
    base = jnp.where(dist >= 0, -dist.astype(F32), NEG_INF)
    for hh in range(MOBA_PAIR):
        bias_ref[hh] = base * slopes_ref[hp * MOBA_PAIR + hh]

    kf = k_ref[...].astype(F32)
    kmean = jnp.mean(kf.reshape(nb, blk, V7X_LANES), axis=1)
    rowid = lax.broadcasted_iota(jnp.int32, (nb, blk), 0)

    for j in range(nb):
        nk = (j + 1) * blk
        qj = q_ref[j * blk:(j + 1) * blk, :]
        outs = []
        for hh in range(MOBA_PAIR):
            in_head = (lane >= hh * MOBA_HEAD_DIM) & (lane < (hh + 1) * MOBA_HEAD_DIM)
            qh = jnp.where(in_head, qj, jnp.zeros_like(qj))
            st = _nt_dot(k_ref[0:nk, :], qh)
            st = st + bias_ref[hh, (nb - 1 - j) * blk:seq, :]
            if j > MOBA_TOPK:
                km = jnp.where(in_head, kmean, 0.0)
                km_hi = km.astype(BF16)
                km_lo = (km - km_hi.astype(F32)).astype(BF16)
                gate = _nt_dot(km_hi, qh) + _nt_dot(km_lo, qh)
                valid = rowid < j
                tiles = []
                for n in range(j):
                    gn = gate[n:n + 1, :]
                    beats = (gate > gn) | ((gate == gn) & (rowid < n))
                    cnt = jnp.sum(jnp.where(valid & beats, 1.0, 0.0), axis=0, keepdims=True)
                    madd = jnp.where(cnt < MOBA_TOPK, 0.0, NEG_INF)
                    tiles.append(st[n * blk:(n + 1) * blk, :] + madd)
                tiles.append(st[j * blk:nk, :])
                st = jnp.concatenate(tiles, axis=0)
            m = jnp.max(st, axis=0, keepdims=True)
            p = jnp.exp(st - m)
            l = jnp.sum(p, axis=0, keepdims=True)
            ot = jnp.dot(vt_ref[:, 0:nk], p.astype(BF16), preferred_element_type=F32)
            outs.append((ot / l).T)
        o = jnp.where(lane < MOBA_HEAD_DIM, outs[0], outs[1])
        o_ref[j * blk:(j + 1) * blk, :] = o.astype(BF16)


def _moba(slopes, qkv, bsz, s):
    t = qkv.shape[0]
    n_pairs = MOBA_HEADS // MOBA_PAIR
    return pl.pallas_call(
        functools.partial(_moba_kernel, seq=s),
        grid_spec=pltpu.PrefetchScalarGridSpec(
            num_scalar_prefetch=1,
            grid=(bsz, n_pairs),
            in_specs=[
                pl.BlockSpec((s, V7X_LANES), lambda b, p, sl: (b, p)),
                pl.BlockSpec((s, V7X_LANES), lambda b, p, sl: (b, n_pairs + p)),
                pl.BlockSpec((s, V7X_LANES), lambda b, p, sl: (b, 2 * n_pairs + p)),
            ],
            out_specs=pl.BlockSpec((s, V7X_LANES), lambda b, p, sl: (b, p)),
            scratch_shapes=[
                pltpu.VMEM((V7X_LANES, s), BF16),
                pltpu.VMEM((MOBA_PAIR, s, MOBA_BLOCK), F32),
            ],
        ),
        out_shape=jax.ShapeDtypeStruct((t, MOBA_WIDTH), BF16),
        name="moba",
        compiler_params=_params("arbitrary", "arbitrary"),
    )(slopes, qkv, qkv, qkv)


XATTN_TQ = 512


def _xattn_kernel(q_ref, k_ref, vt_ref, o_ref):
    scale = XATTN_HEAD_DIM ** -0.5
    for h in range(XATTN_HEADS):
        lo = h * XATTN_HEAD_DIM
        st = _nt_dot(k_ref[0, :, lo:lo + XATTN_HEAD_DIM], q_ref[:, lo:lo + XATTN_HEAD_DIM]) * scale
        m = jnp.max(st, axis=0, keepdims=True)
        p = jnp.exp(st - m)
        l = jnp.sum(p, axis=0, keepdims=True)
        ot = jnp.dot(vt_ref[0, lo:lo + XATTN_HEAD_DIM, :], p.astype(BF16), preferred_element_type=F32)
        o_ref[:, lo:lo + XATTN_HEAD_DIM] = (ot / l).T.astype(BF16)


def _xattn(qkv, mk, mvt, bsz, s):
    t = qkv.shape[0]
    nt = s // XATTN_TQ
    m = mk.shape[1]
    qx_block = 3 * MOBA_WIDTH // XATTN_WIDTH
    return pl.pallas_call(
        _xattn_kernel,
        grid=(bsz, nt),
        in_specs=[
            pl.BlockSpec((XATTN_TQ, XATTN_WIDTH), lambda b, j: (b * nt + j, qx_block)),
            pl.BlockSpec((1, m, XATTN_WIDTH), lambda b, j: (b, 0, 0)),
            pl.BlockSpec((1, XATTN_WIDTH, m), lambda b, j: (b, 0, 0)),
        ],
        out_specs=pl.BlockSpec((XATTN_TQ, XATTN_WIDTH), lambda b, j: (b * nt + j, 0)),
        out_shape=jax.ShapeDtypeStruct((t, XATTN_WIDTH), BF16),
        name="xattn",
        compiler_params=_params("arbitrary", "arbitrary"),
    )(qkv, mk, mvt)


MERGE_TM = 512


def _merge_kernel(x_ref, g_ref, ya_ref, yb_ref, yc_ref, wg_ref, wb_ref, wo_ref, o_ref):
    x = x_ref[...]
    h = _rms(x, g_ref[...]).astype(BF16)
    merged = None
    for n, y_ref in enumerate((ya_ref, yb_ref, yc_ref)):
        logits = jnp.dot(h, wg_ref[:, n * D_MODEL:(n + 1) * D_MODEL], preferred_element_type=F32)
        branch = jnp.dot(y_ref[...], wb_ref[n], preferred_element_type=F32)
        term = jax.nn.sigmoid(logits) * branch
        merged = term if merged is None else merged + term
    o_ref[...] = x + jnp.dot(merged.astype(BF16), wo_ref[...], preferred_element_type=F32)


def _merge(x2, gain, ya, yb, yc, w_gate, w_branch, w_out):
    t, d = x2.shape
    w = LRU_WIDTH
    return pl.pallas_call(
        _merge_kernel,
        grid=(t // MERGE_TM,),
        in_specs=[
            pl.BlockSpec((MERGE_TM, d), lambda i: (i, 0)),
            pl.BlockSpec((1, d), lambda i: (0, 0)),
            pl.BlockSpec((MERGE_TM, w), lambda i: (i, 0)),
            pl.BlockSpec((MERGE_TM, w), lambda i: (i, 0)),
            pl.BlockSpec((MERGE_TM, w), lambda i: (i, 0)),
            pl.BlockSpec((d, N_BRANCH * d), lambda i: (0, 0)),
            pl.BlockSpec((N_BRANCH, w, d), lambda i: (0, 0, 0)),
            pl.BlockSpec((d, d), lambda i: (0, 0)),
        ],
        out_specs=pl.BlockSpec((MERGE_TM, d), lambda i: (i, 0)),
        out_shape=jax.ShapeDtypeStruct((t, d), F32),
        name="merge",
        compiler_params=_params("arbitrary"),
    )(x2, gain, ya, yb, yc, w_gate, w_branch, w_out)


FFN_TM = 512
FFN_TF = 512


def _ffn_kernel(x_ref, g_ref, wg_ref, wu_ref, cw_ref, cb_ref, wd_ref, fg_ref, o_ref,
                h_ref, acc_ref, gs_ref, tail_ref, *, tiles_per_seq, final_norm):
    tm = FFN_TM
    i = pl.program_id(0)
    c = pl.program_id(1)
    seq_start = (i % tiles_per_seq) == 0

    @pl.when(c == 0)
    def _():
        h_ref[...] = _rms(x_ref[...], g_ref[...]).astype(BF16)
        acc_ref[...] = jnp.zeros_like(acc_ref)

    h = h_ref[...]
    g = jnp.dot(h, wg_ref[...], preferred_element_type=F32)
    u = jnp.dot(h, wu_ref[...], preferred_element_type=F32)

    @pl.when(seq_start)
    def _():
        gs_ref[0:V7X_SUBLANES, :] = jnp.zeros((V7X_SUBLANES, FFN_TF), F32)

    @pl.when(jnp.logical_not(seq_start))
    def _():
        gs_ref[0:V7X_SUBLANES, :] = tail_ref[c]

    gs_ref[V7X_SUBLANES:V7X_SUBLANES + tm, :] = g
    tail_ref[c] = g[tm - V7X_SUBLANES:tm, :]
    conv = cb_ref[...] + cw_ref[FFN_CONV - 1:FFN_CONV, :] * g
    for k in range(FFN_CONV - 1):
        off = V7X_SUBLANES - (FFN_CONV - 1) + k
        conv = conv + cw_ref[k:k + 1, :] * gs_ref[off:off + tm, :]
    act = (jax.nn.gelu(conv) * u).astype(BF16)
    acc_ref[...] += jnp.dot(act, wd_ref[...], preferred_element_type=F32)

    @pl.when(c == pl.num_programs(1) - 1)
    def _():
        y = x_ref[...] + acc_ref[...]
        if final_norm:
            y = _rms(y, fg_ref[...])
        o_ref[...] = y


def _ffn(x2, gain, w_gate, w_up, conv_w, conv_b, w_down, final_gain, s, final_norm):
    t, d = x2.shape
    nc = D_FF // FFN_TF
    kern = functools.partial(_ffn_kernel, tiles_per_seq=s // FFN_TM, final_norm=final_norm)
    return pl.pallas_call(
        kern,
        grid=(t // FFN_TM, nc),
        in_specs=[
            pl.BlockSpec((FFN_TM, d), lambda i, c: (i, 0)),
            pl.BlockSpec((1, d), lambda i, c: (0, 0)),
            pl.BlockSpec((d, FFN_TF), lambda i, c: (0, c)),
            pl.BlockSpec((d, FFN_TF), lambda i, c: (0, c)),
            pl.BlockSpec((FFN_CONV, FFN_TF), lambda i, c: (0, c)),
            pl.BlockSpec((1, FFN_TF), lambda i, c: (0, c)),
            pl.BlockSpec((FFN_TF, d), lambda i, c: (c, 0)),
            pl.BlockSpec((1, d), lambda i, c: (0, 0)),
        ],
        out_specs=pl.BlockSpec((FFN_TM, d), lambda i, c: (i, 0)),
        out_shape=jax.ShapeDtypeStruct((t, d), F32),
        scratch_shapes=[
            pltpu.VMEM((FFN_TM, d), BF16),
            pltpu.VMEM((FFN_TM, d), F32),
            pltpu.VMEM((FFN_TM + V7X_SUBLANES, FFN_TF), F32),
            pltpu.VMEM((nc, V7X_SUBLANES, FFN_TF), F32),
        ],
        name="ffn_final" if final_norm else "ffn",
        compiler_params=_params("arbitrary", "arbitrary"),
    )(x2, gain, w_gate, w_up, conv_w, conv_b, w_down, final_gain)


def _block_diag(w):
    nh, hd, _ = w.shape
    eye = jnp.eye(nh, dtype=w.dtype)
    return (eye[:, None, :, None] * w[:, :, None, :]).reshape(nh * hd, nh * hd)


def kernel(x, mem, mix_norm_gain, w_in, lru_conv_w, lru_conv_b, lru_w_a, lru_b_a, lru_w_x, lru_b_x, lru_lambda,
           mem_norm_gain, w_mem_kv, w_branch, w_out, ffn_norm_gain, w_ffn_gate, w_ffn_up, ffn_conv_w, ffn_conv_b,
           w_ffn_down, final_norm_gain):
    bsz, s, d = x.shape
    depth = w_in.shape[0]
    x2 = x.reshape(bsz * s, d)
    slopes = jnp.exp2(-8.0 * jnp.arange(1, MOBA_HEADS + 1, dtype=F32) / MOBA_HEADS)
    row = lambda v: v.reshape(1, -1)
    for l in range(depth):
        w_proj = w_in[l, :, :PROJ_COLS].astype(BF16)
        w_glog = w_in[l, :, PROJ_COLS:].astype(BF16)
        w_gates = jnp.concatenate([_block_diag(lru_w_a[l]), _block_diag(lru_w_x[l])], axis=1).astype(BF16)
        b_gates = jnp.concatenate([lru_b_a[l].reshape(-1), lru_b_x[l].reshape(-1)]).reshape(1, -1)

        mk, mvt = _memkv(mem, row(mem_norm_gain[l]), w_mem_kv[l].astype(BF16))
        xg, qkv = _inproj(x2, row(mix_norm_gain[l]), w_proj)
        ya = _lru(xg, lru_conv_w[l], row(lru_conv_b[l]), w_gates, b_gates, row(lru_lambda[l]), bsz, s)
        yb = _moba(slopes, qkv, bsz, s)
        yc = _xattn(qkv, mk, mvt, bsz, s)
        x2 = _merge(x2, row(mix_norm_gain[l]), ya, yb, yc, w_glog, w_branch[l].astype(BF16), w_out[l].astype(BF16))
        x2 = _ffn(x2, row(ffn_norm_gain[l]), w_ffn_gate[l].astype(BF16), w_ffn_up[l].astype(BF16),
                  ffn_conv_w[l], row(ffn_conv_b[l]), w_ffn_down[l].astype(BF16), row(final_norm_gain),
                  s, final_norm=(l == depth - 1))
    return x2.reshape(bsz, s, d)
```

```python
import functools

import jax
import jax.numpy as jnp
from jax import lax
from jax.experimental import pallas as pl
from jax.experimental.pallas import tpu as pltpu

F32 = jnp.float32
BF16 = jnp.bfloat16

D_MODEL = 1024
LRU_WIDTH = 512
LRU_HEADS = 8
LRU_HEAD_DIM = LRU_WIDTH // LRU_HEADS
LRU_CONV = 4
LRU_C = 8.0
MOBA_HEADS = 8
MOBA_HEAD_DIM = 64
MOBA_WIDTH = MOBA_HEADS * MOBA_HEAD_DIM
MOBA_BLOCK = 256
MOBA_TOPK = 3
XATTN_HEADS = 4
XATTN_HEAD_DIM = 128
XATTN_WIDTH = XATTN_HEADS * XATTN_HEAD_DIM
N_BRANCH = 3
D_FF = 3 * D_MODEL
FFN_CONV = 3
NORM_EPS = 1e-6

V7X_SUBLANES = 8
V7X_LANES = 128
VMEM_LIMIT_BYTES = 56 * 1024 * 1024

PROJ_COLS = 2 * LRU_WIDTH + 3 * MOBA_WIDTH + XATTN_WIDTH
NEG_INF = float("-inf")


def _params(*semantics):
    return pltpu.CompilerParams(dimension_semantics=semantics, vmem_limit_bytes=VMEM_LIMIT_BYTES)


def _rms(x, g):
    return x * lax.rsqrt(jnp.mean(x * x, axis=-1, keepdims=True) + NORM_EPS) * g


def _nt_dot(a, b):
    return lax.dot_general(a, b, (((1,), (1,)), ((), ())), preferred_element_type=F32)


def _memkv_kernel(mem_ref, g_ref, w_ref, k_ref, vt_ref):
    h = _rms(mem_ref[0], g_ref[...]).astype(BF16)
    k_ref[0] = jnp.dot(h, w_ref[:, :XATTN_WIDTH], preferred_element_type=F32).astype(BF16)
    v = jnp.dot(h, w_ref[:, XATTN_WIDTH:], preferred_element_type=F32)
    vt_ref[0] = v.T.astype(BF16)


def _memkv(mem, gain, w):
    bsz, m, d = mem.shape
    return pl.pallas_call(
        _memkv_kernel,
        grid=(bsz,),
        in_specs=[
            pl.BlockSpec((1, m, d), lambda b: (b, 0, 0)),
            pl.BlockSpec((1, d), lambda b: (0, 0)),
            pl.BlockSpec((d, 2 * XATTN_WIDTH), lambda b: (0, 0)),
        ],
        out_specs=[
            pl.BlockSpec((1, m, XATTN_WIDTH), lambda b: (b, 0, 0)),
            pl.BlockSpec((1, XATTN_WIDTH, m), lambda b: (b, 0, 0)),
        ],
        out_shape=[
            jax.ShapeDtypeStruct((bsz, m, XATTN_WIDTH), BF16),
            jax.ShapeDtypeStruct((bsz, XATTN_WIDTH, m), BF16),
        ],
        name="memkv",
        compiler_params=_params("arbitrary"),
    )(mem, gain, w)


INPROJ_TM = 512
INPROJ_CHUNK = 512


def _inproj_kernel(x_ref, g_ref, w_ref, xg_ref, qkv_ref):
    h = _rms(x_ref[...], g_ref[...]).astype(BF16)
    n_f32 = 2 * LRU_WIDTH // INPROJ_CHUNK
    for c in range(PROJ_COLS // INPROJ_CHUNK):
        lo = c * INPROJ_CHUNK
        y = jnp.dot(h, w_ref[:, lo:lo + INPROJ_CHUNK], preferred_element_type=F32)
        if c < n_f32:
            xg_ref[:, lo:lo + INPROJ_CHUNK] = y
        else:
            if c == n_f32:
                y = y * (MOBA_HEAD_DIM ** -0.5)
            o = lo - 2 * LRU_WIDTH
            qkv_ref[:, o:o + INPROJ_CHUNK] = y.astype(BF16)


def _inproj(x2, gain, w):
    t, d = x2.shape
    n_bf = PROJ_COLS - 2 * LRU_WIDTH
    return pl.pallas_call(
        _inproj_kernel,
        grid=(t // INPROJ_TM,),
        in_specs=[
            pl.BlockSpec((INPROJ_TM, d), lambda i: (i, 0)),
            pl.BlockSpec((1, d), lambda i: (0, 0)),
            pl.BlockSpec((d, PROJ_COLS), lambda i: (0, 0)),
        ],
        out_specs=[
            pl.BlockSpec((INPROJ_TM, 2 * LRU_WIDTH), lambda i: (i, 0)),
            pl.BlockSpec((INPROJ_TM, n_bf), lambda i: (i, 0)),
        ],
        out_shape=[
            jax.ShapeDtypeStruct((t, 2 * LRU_WIDTH), F32),
            jax.ShapeDtypeStruct((t, n_bf), BF16),
        ],
        name="inproj",
        compiler_params=_params("arbitrary"),
    )(x2, gain, w)


LRU_TM = 512
LRU_UNROLL = 4


def _lru_kernel(xg_ref, cw_ref, cb_ref, wg_ref, bg_ref, lam_ref, y_ref,
                xs_ref, a_ref, u_ref, h_ref, carry_ref):
    tm = LRU_TM
    w = LRU_WIDTH
    first = pl.program_id(1) == 0

    @pl.when(first)
    def _():
        xs_ref[0:V7X_SUBLANES, :] = jnp.zeros((V7X_SUBLANES, w), F32)
        carry_ref[...] = jnp.zeros((V7X_SUBLANES, w), F32)

    @pl.when(jnp.logical_not(first))
    def _():
        xs_ref[0:V7X_SUBLANES, :] = xs_ref[tm:tm + V7X_SUBLANES, :]

    xs_ref[V7X_SUBLANES:V7X_SUBLANES + tm, :] = xg_ref[:, :w]

    xc = cb_ref[...] + cw_ref[LRU_CONV - 1:LRU_CONV, :] * xs_ref[V7X_SUBLANES:V7X_SUBLANES + tm, :]
    for k in range(LRU_CONV - 1):
        off = V7X_SUBLANES - (LRU_CONV - 1) + k
        xc = xc + cw_ref[k:k + 1, :] * xs_ref[off:off + tm, :]

    gates = jnp.dot(xc.astype(BF16), wg_ref[...], preferred_element_type=F32) + bg_ref[...]
    r = jax.nn.sigmoid(gates[:, :w])
    i = jax.nn.sigmoid(gates[:, w:])
    z = -lam_ref[...]
    softplus = jnp.maximum(z, 0.0) + jnp.log1p(jnp.exp(-jnp.abs(z)))
    log_a = (-LRU_C) * r * softplus
    a = jnp.exp(log_a)
    th = jnp.tanh(log_a)
    mult = jnp.sqrt(-2.0 * th / (1.0 - th))
    a_ref[...] = a
    u_ref[...] = mult * i * xc

    sub = lax.broadcasted_iota(jnp.int32, (V7X_SUBLANES, w), 0)

    def body(g, carry):
        r0 = pl.multiple_of(g * V7X_SUBLANES, V7X_SUBLANES)
        av = a_ref[pl.ds(r0, V7X_SUBLANES), :]
        uv = u_ref[pl.ds(r0, V7X_SUBLANES), :]
        for d in (1, 2, 4):
            keep = sub >= d
            a_s = jnp.where(keep, pltpu.roll(av, d, 0), 1.0)
            u_s = jnp.where(keep, pltpu.roll(uv, d, 0), 0.0)
            uv = uv + av * u_s
            av = av * a_s
        hv = uv + av * carry
        h_ref[pl.ds(r0, V7X_SUBLANES), :] = hv
        return jnp.broadcast_to(hv[V7X_SUBLANES - 1:V7X_SUBLANES, :], (V7X_SUBLANES, w))

    carry_ref[...] = lax.fori_loop(0, tm // V7X_SUBLANES, body, carry_ref[...], unroll=LRU_UNROLL)
    y_ref[...] = (h_ref[...] * jax.nn.gelu(xg_ref[:, w:])).astype(BF16)


def _lru(xg, conv_w, conv_b, w_gates, b_gates, lam, bsz, s):
    t = xg.shape[0]
    w = LRU_WIDTH
    nt = s // LRU_TM
    return pl.pallas_call(
        _lru_kernel,
        grid=(bsz, nt),
        in_specs=[
            pl.BlockSpec((LRU_TM, 2 * w), lambda b, j: (b * nt + j, 0)),
            pl.BlockSpec((LRU_CONV, w), lambda b, j: (0, 0)),
            pl.BlockSpec((1, w), lambda b, j: (0, 0)),
            pl.BlockSpec((w, 2 * w), lambda b, j: (0, 0)),
            pl.BlockSpec((1, 2 * w), lambda b, j: (0, 0)),
            pl.BlockSpec((1, w), lambda b, j: (0, 0)),
        ],
        out_specs=pl.BlockSpec((LRU_TM, w), lambda b, j: (b * nt + j, 0)),
        out_shape=jax.ShapeDtypeStruct((t, w), BF16),
        scratch_shapes=[
            pltpu.VMEM((LRU_TM + 2 * V7X_SUBLANES, w), F32),
            pltpu.VMEM((LRU_TM, w), F32),
            pltpu.VMEM((LRU_TM, w), F32),
            pltpu.VMEM((LRU_TM, w), F32),
            pltpu.VMEM((V7X_SUBLANES, w), F32),
        ],
        name="lru",
        compiler_params=_params("arbitrary", "arbitrary"),
    )(xg, conv_w, conv_b, w_gates, b_gates, lam)


MOBA_PAIR = V7X_LANES // MOBA_HEAD_DIM


MOBA_AUX_POS, MOBA_AUX_BLK, MOBA_AUX_ONE = 0, 1, 2
MOBA_AUX_SEL = V7X_SUBLANES
MOBA_MASKED = -1e30


def _moba_kernel(slopes_ref, q_ref, k_ref, v_ref, o_ref, vt_ref, kw_ref, *, seq):
    nb = seq // MOBA_BLOCK
    blk = MOBA_BLOCK
    wide = MOBA_PAIR * blk
    assert MOBA_PAIR == 2 and nb <= V7X_SUBLANES and blk & (blk - 1) == 0
    hp = pl.program_id(1)
    lane = lax.broadcasted_iota(jnp.int32, (1, V7X_LANES), 1)

    @pl.when((pl.program_id(0) == 0) & (hp == 0))
    def _():
        rk = lax.broadcasted_iota(jnp.int32, (seq, V7X_LANES), 0)
        lk = lax.broadcasted_iota(jnp.int32, (seq, V7X_LANES), 1)
        pos = jnp.bitwise_and(rk, blk - 1)
        blk_idx = lax.shift_right_logical(rk, blk.bit_length() - 1)
        aux = jnp.where(lk == MOBA_AUX_POS, pos.astype(F32),
                        jnp.where(lk == MOBA_AUX_BLK, (rk - pos).astype(F32),
                                  jnp.where((lk == MOBA_AUX_ONE) | (lk - MOBA_AUX_SEL == blk_idx), 1.0, 0.0)))
        kw_ref[:, V7X_LANES:] = aux.astype(BF16)

    vt_ref[...] = v_ref[...].astype(F32).T.astype(BF16)
    kb = k_ref[...]
    kw_ref[:, :V7X_LANES] = kb
    kmean = jnp.mean(kb.astype(F32).reshape(nb, blk, V7X_LANES), axis=1)
    km_hi = kmean.astype(BF16)
    km_lo = (kmean - km_hi.astype(F32)).astype(BF16)

    rowid = lax.broadcasted_iota(jnp.int32, (V7X_SUBLANES, wide), 0)
    colid = lax.broadcasted_iota(jnp.int32, (V7X_SUBLANES, wide), 1)
    slope = jnp.where(colid < blk, slopes_ref[hp * MOBA_PAIR], slopes_ref[hp * MOBA_PAIR + 1])
    key_row = lax.broadcasted_iota(jnp.int32, (blk, wide), 0)
    qry_col = jnp.bitwise_and(lax.broadcasted_iota(jnp.int32, (blk, wide), 1), blk - 1)
    causal = key_row <= qry_col

    def scores(j):
        qj = q_ref[j * blk:(j + 1) * blk, :]
        zero = jnp.zeros_like(qj)
        q_heads = jnp.concatenate([jnp.where(lane < MOBA_HEAD_DIM, qj, zero),
                                   jnp.where(lane < MOBA_HEAD_DIM, zero, qj)], axis=0)
        q_rows = jnp.where(rowid <= MOBA_AUX_BLK, slope,
                           jnp.where(rowid == MOBA_AUX_ONE, -slope * (j * blk), 0.0))
        sel_rows = jnp.zeros((V7X_SUBLANES, wide), F32)
        if j > MOBA_TOPK:
            gate = _nt_dot(km_hi, q_heads) + _nt_dot(km_lo, q_heads)
            if nb < V7X_SUBLANES:
                gate = jnp.concatenate([gate, jnp.zeros((V7X_SUBLANES - nb, wide), F32)], axis=0)
            cnt = jnp.zeros((V7X_SUBLANES, wide), F32)
            for mth in range(j):
                gm = gate[mth:mth + 1, :]
                beats = (gm > gate) | ((gm == gate) & (rowid > mth))
                cnt = cnt + jnp.where(beats, 1.0, 0.0)
            sel_rows = jnp.where((rowid < j) & (cnt >= MOBA_TOPK), MOBA_MASKED, 0.0)
        q_aux = jnp.concatenate(
            [q_rows, sel_rows, jnp.zeros((V7X_LANES - 2 * V7X_SUBLANES, wide), F32)], axis=0).T.astype(BF16)
        q_wide = jnp.concatenate([q_heads, q_aux], axis=1)
        st_own = _nt_dot(kw_ref[j * blk:(j + 1) * blk, :], q_wide)
        st_own = jnp.where(causal, st_own, NEG_INF)
        m = jnp.max(st_own, axis=0, keepdims=True)
        st_past = None
        if j > 0:
            st_past = _nt_dot(kw_ref[0:j * blk, :], q_wide)
            m = jnp.maximum(m, jnp.max(st_past, axis=0, keepdims=True))
        return st_own, st_past, m

    def finish(j, st_own, st_past, m):
        p_own = jnp.exp(st_own - m)
        l = jnp.sum(p_own, axis=0, keepdims=True)
        ot = jnp.dot(vt_ref[:, j * blk:(j + 1) * blk], p_own.astype(BF16), preferred_element_type=F32)
        if j > 0:
            p_past = jnp.exp(st_past - m)
            l = l + jnp.sum(p_past, axis=0, keepdims=True)
            ot = ot + jnp.dot(vt_ref[:, 0:j * blk], p_past.astype(BF16), preferred_element_type=F32)
        ot = ot / l
        o_t = jnp.concatenate([ot[:MOBA_HEAD_DIM, :blk], ot[MOBA_HEAD_DIM:, blk:]], axis=0)
        o_ref[j * blk:(j + 1) * blk, :] = o_t.T.astype(BF16)

    pending = scores(0)
    for j in range(nb):
        upcoming = scores(j + 1) if j + 1 < nb else None
        finish(j, *pending)
        pending = upcoming


def _moba(slopes, qkv, bsz, s):
    t = qkv.shape[0]
    n_pairs = MOBA_HEADS // MOBA_PAIR
    return pl.pallas_call(
        functools.partial(_moba_kernel, seq=s),
        grid_spec=pltpu.PrefetchScalarGridSpec(
            num_scalar_prefetch=1,
            grid=(bsz, n_pairs),
            in_specs=[
                pl.BlockSpec((s, V7X_LANES), lambda b, p, sl: (b, p)),
                pl.BlockSpec((s, V7X_LANES), lambda b, p, sl: (b, n_pairs + p)),
                pl.BlockSpec((s, V7X_LANES), lambda b, p, sl: (b, 2 * n_pairs + p)),
            ],
            out_specs=pl.BlockSpec((s, V7X_LANES), lambda b, p, sl: (b, p)),
            scratch_shapes=[
                pltpu.VMEM((V7X_LANES, s), BF16),
                pltpu.VMEM((s, 2 * V7X_LANES), BF16),
            ],
        ),
        out_shape=jax.ShapeDtypeStruct((t, MOBA_WIDTH), BF16),
        name="moba",
        compiler_params=_params("arbitrary", "arbitrary"),
    )(slopes, qkv, qkv, qkv)


XATTN_TQ = 512


def _xattn_kernel(q_ref, k_ref, vt_ref, o_ref):
    scale = XATTN_HEAD_DIM ** -0.5
    for h in range(XATTN_HEADS):
        lo = h * XATTN_HEAD_DIM
        st = _nt_dot(k_ref[0, :, lo:lo + XATTN_HEAD_DIM], q_ref[:, lo:lo + XATTN_HEAD_DIM]) * scale
        m = jnp.max(st, axis=0, keepdims=True)
        p = jnp.exp(st - m)
        l = jnp.sum(p, axis=0, keepdims=True)
        ot = jnp.dot(vt_ref[0, lo:lo + XATTN_HEAD_DIM, :], p.astype(BF16), preferred_element_type=F32)
        o_ref[:, lo:lo + XATTN_HEAD_DIM] = (ot / l).T.astype(BF16)


def _xattn(qkv, mk, mvt, bsz, s):
    t = qkv.shape[0]
    nt = s // XATTN_TQ
    m = mk.shape[1]
    qx_block = 3 * MOBA_WIDTH // XATTN_WIDTH
    return pl.pallas_call(
        _xattn_kernel,
        grid=(bsz, nt),
        in_specs=[
            pl.BlockSpec((XATTN_TQ, XATTN_WIDTH), lambda b, j: (b * nt + j, qx_block)),
            pl.BlockSpec((1, m, XATTN_WIDTH), lambda b, j: (b, 0, 0)),
            pl.BlockSpec((1, XATTN_WIDTH, m), lambda b, j: (b, 0, 0)),
        ],
        out_specs=pl.BlockSpec((XATTN_TQ, XATTN_WIDTH), lambda b, j: (b * nt + j, 0)),
        out_shape=jax.ShapeDtypeStruct((t, XATTN_WIDTH), BF16),
        name="xattn",
        compiler_params=_params("arbitrary", "arbitrary"),
    )(qkv, mk, mvt)


MERGE_TM = 512


def _merge_kernel(x_ref, g_ref, ya_ref, yb_ref, yc_ref, wg_ref, wb_ref, wo_ref, o_ref):
    x = x_ref[...]
    h = _rms(x, g_ref[...]).astype(BF16)
    merged = None
    for n, y_ref in enumerate((ya_ref, yb_ref, yc_ref)):
        logits = jnp.dot(h, wg_ref[:, n * D_MODEL:(n + 1) * D_MODEL], preferred_element_type=F32)
        branch = jnp.dot(y_ref[...], wb_ref[n], preferred_element_type=F32)
        term = jax.nn.sigmoid(logits) * branch
        merged = term if merged is None else merged + term
    o_ref[...] = x + jnp.dot(merged.astype(BF16), wo_ref[...], preferred_element_type=F32)


def _merge(x2, gain, ya, yb, yc, w_gate, w_branch, w_out):
    t, d = x2.shape
    w = LRU_WIDTH
    return pl.pallas_call(
        _merge_kernel,
        grid=(t // MERGE_TM,),
        in_specs=[
            pl.BlockSpec((MERGE_TM, d), lambda i: (i, 0)),
            pl.BlockSpec((1, d), lambda i: (0, 0)),
            pl.BlockSpec((MERGE_TM, w), lambda i: (i, 0)),
            pl.BlockSpec((MERGE_TM, w), lambda i: (i, 0)),
            pl.BlockSpec((MERGE_TM, w), lambda i: (i, 0)),
            pl.BlockSpec((d, N_BRANCH * d), lambda i: (0, 0)),
            pl.BlockSpec((N_BRANCH, w, d), lambda i: (0, 0, 0)),
            pl.BlockSpec((d, d), lambda i: (0, 0)),
        ],
        out_specs=pl.BlockSpec((MERGE_TM, d), lambda i: (i, 0)),
        out_shape=jax.ShapeDtypeStruct((t, d), F32),
        name="merge",
        compiler_params=_params("arbitrary"),
    )(x2, gain, ya, yb, yc, w_gate, w_branch, w_out)


FFN_TM = 512
FFN_TF = 1024
FFN_SUB = 256


def _ffn_kernel(x_ref, g_ref, wg_ref, wu_ref, cw_ref, cb_ref, wd_ref, fg_ref, o_ref,
                h_ref, tail_ref, *, tiles_per_seq, final_norm):
    tm = FFN_TM
    i = pl.program_id(0)
    c = pl.program_id(1)

    @pl.when((i == 0) & (c == 0))
    def _():
        tail_ref[...] = jnp.zeros_like(tail_ref)

    @pl.when(c == 0)
    def _():
        x = x_ref[...]
        h_ref[...] = _rms(x, g_ref[...]).astype(BF16)
        o_ref[...] = x

    h = h_ref[...]
    sub = lax.broadcasted_iota(jnp.int32, (V7X_SUBLANES, FFN_SUB), 0)
    mid_seq = jnp.broadcast_to(i % tiles_per_seq, (V7X_SUBLANES, FFN_SUB)) != 0
    acts = []
    for k in range(FFN_TF // FFN_SUB):
        lo = k * FFN_SUB
        g = jnp.dot(h, wg_ref[:, lo:lo + FFN_SUB], preferred_element_type=F32)
        u = jnp.dot(h, wu_ref[:, lo:lo + FFN_SUB], preferred_element_type=F32)
        prev = jnp.where(mid_seq, tail_ref[c, :, lo:lo + FFN_SUB], 0.0)
        tail_ref[c, :, lo:lo + FFN_SUB] = g[tm - V7X_SUBLANES:tm, :]
        conv = cb_ref[:, lo:lo + FFN_SUB] + cw_ref[FFN_CONV - 1:FFN_CONV, lo:lo + FFN_SUB] * g
        for shift in range(1, FFN_CONV):
            rolled = pltpu.roll(g, shift, 0)
            head = jnp.where(sub < shift, pltpu.roll(prev, shift, 0), rolled[0:V7X_SUBLANES, :])
            shifted = jnp.concatenate([head, rolled[V7X_SUBLANES:, :]], axis=0)
            tap = FFN_CONV - 1 - shift
            conv = conv + cw_ref[tap:tap + 1, lo:lo + FFN_SUB] * shifted
        acts.append((jax.nn.gelu(conv) * u).astype(BF16))
    act = jnp.concatenate(acts, axis=1)
    o_ref[...] += jnp.dot(act, wd_ref[...], preferred_element_type=F32)

    if final_norm:
        @pl.when(c == pl.num_programs(1) - 1)
        def _():
            o_ref[...] = _rms(o_ref[...], fg_ref[...])


def _ffn(x2, gain, w_gate, w_up, conv_w, conv_b, w_down, final_gain, s, final_norm):
    t, d = x2.shape
    nc = D_FF // FFN_TF
    kern = functools.partial(_ffn_kernel, tiles_per_seq=s // FFN_TM, final_norm=final_norm)
    return pl.pallas_call(
        kern,
        grid=(t // FFN_TM, nc),
        in_specs=[
            pl.BlockSpec((FFN_TM, d), lambda i, c: (i, 0)),
            pl.BlockSpec((1, d), lambda i, c: (0, 0)),
            pl.BlockSpec((d, FFN_TF), lambda i, c: (0, c)),
            pl.BlockSpec((d, FFN_TF), lambda i, c: (0, c)),
            pl.BlockSpec((FFN_CONV, FFN_TF), lambda i, c: (0, c)),
            pl.BlockSpec((1, FFN_TF), lambda i, c: (0, c)),
            pl.BlockSpec((FFN_TF, d), lambda i, c: (c, 0)),
            pl.BlockSpec((1, d), lambda i, c: (0, 0)),
        ],
        out_specs=pl.BlockSpec((FFN_TM, d), lambda i, c: (i, 0)),
        out_shape=jax.ShapeDtypeStruct((t, d), F32),
        scratch_shapes=[
            pltpu.VMEM((FFN_TM, d), BF16),
            pltpu.VMEM((nc, V7X_SUBLANES, FFN_TF), F32),
        ],
        name="ffn_final" if final_norm else "ffn",
        compiler_params=_params("arbitrary", "arbitrary"),
    )(x2, gain, w_gate, w_up, conv_w, conv_b, w_down, final_gain)


def _block_diag(w):
    nh, hd, _ = w.shape
    eye = jnp.eye(nh, dtype=w.dtype)
    return (eye[:, None, :, None] * w[:, :, None, :]).reshape(nh * hd, nh * hd)


def kernel(x, mem, mix_norm_gain, w_in, lru_conv_w, lru_conv_b, lru_w_a, lru_b_a, lru_w_x, lru_b_x, lru_lambda,
           mem_norm_gain, w_mem_kv, w_branch, w_out, ffn_norm_gain, w_ffn_gate, w_ffn_up, ffn_conv_w, ffn_conv_b,
           w_ffn_down, final_norm_gain):
    bsz, s, d = x.shape
    depth = w_in.shape[0]
    x2 = x.reshape(bsz * s, d)
    slopes = jnp.exp2(-8.0 * jnp.arange(1, MOBA_HEADS + 1, dtype=F32) / MOBA_HEADS)
    row = lambda v: v.reshape(1, -1)
    for l in range(depth):
        w_proj = w_in[l, :, :PROJ_COLS].astype(BF16)
        w_glog = w_in[l, :, PROJ_COLS:].astype(BF16)
        w_gates = jnp.concatenate([_block_diag(lru_w_a[l]), _block_diag(lru_w_x[l])], axis=1).astype(BF16)
        b_gates = jnp.concatenate([lru_b_a[l].reshape(-1), lru_b_x[l].reshape(-1)]).reshape(1, -1)

        mk, mvt = _memkv(mem, row(mem_norm_gain[l]), w_mem_kv[l].astype(BF16))
        xg, qkv = _inproj(x2, row(mix_norm_gain[l]), w_proj)
        ya = _lru(xg, lru_conv_w[l], row(lru_conv_b[l]), w_gates, b_gates, row(lru_lambda[l]), bsz, s)
        yb = _moba(slopes, qkv, bsz, s)
        yc = _xattn(qkv, mk, mvt, bsz, s)
        x2 = _merge(x2, row(mix_norm_gain[l]), ya, yb, yc, w_glog, w_branch[l].astype(BF16), w_out[l].astype(BF16))
        x2 = _ffn(x2, row(ffn_norm_gain[l]), w_ffn_gate[l].astype(BF16), w_ffn_up[l].astype(BF16),
                  ffn_conv_w[l], row(ffn_conv_b[l]), w_ffn_down[l].astype(BF16), row(final_norm_gain),
                  s, final_norm=(l == depth - 1))
    return x2.reshape(bsz, s, d)
```

```python
import functools

import jax
import jax.numpy as jnp
from jax import lax
from jax.experimental import pallas as pl
from jax.experimental.pallas import tpu as pltpu

F32 = jnp.float32
BF16 = jnp.bfloat16

D_MODEL = 1024
LRU_WIDTH = 512
LRU_HEADS = 8
LRU_HEAD_DIM = LRU_WIDTH // LRU_HEADS
LRU_CONV = 4
LRU_C = 8.0
MOBA_HEADS = 8
MOBA_HEAD_DIM = 64
MOBA_WIDTH = MOBA_HEADS * MOBA_HEAD_DIM
MOBA_BLOCK = 256
MOBA_TOPK = 3
XATTN_HEADS = 4
XATTN_HEAD_DIM = 128
XATTN_WIDTH = XATTN_HEADS * XATTN_HEAD_DIM
N_BRANCH = 3
D_FF = 3 * D_MODEL
FFN_CONV = 3
NORM_EPS = 1e-6

V7X_SUBLANES = 8
V7X_LANES = 128
VMEM_LIMIT_BYTES = 56 * 1024 * 1024

PROJ_COLS = 2 * LRU_WIDTH + 3 * MOBA_WIDTH + XATTN_WIDTH
NEG_INF = float("-inf")


def _params(*semantics):
    return pltpu.CompilerParams(dimension_semantics=semantics, vmem_limit_bytes=VMEM_LIMIT_BYTES)


def _rms(x, g):
    return x * lax.rsqrt(jnp.mean(x * x, axis=-1, keepdims=True) + NORM_EPS) * g


def _nt_dot(a, b):
    return lax.dot_general(a, b, (((1,), (1,)), ((), ())), preferred_element_type=F32)


def _memkv_kernel(mem_ref, g_ref, w_ref, k_ref, vt_ref):
    h = _rms(mem_ref[0], g_ref[...]).astype(BF16)
    k_ref[0] = jnp.dot(h, w_ref[:, :XATTN_WIDTH], preferred_element_type=F32).astype(BF16)
    v = jnp.dot(h, w_ref[:, XATTN_WIDTH:], preferred_element_type=F32)
    vt_ref[0] = v.T.astype(BF16)


def _memkv(mem, gain, w):
    bsz, m, d = mem.shape
    return pl.pallas_call(
        _memkv_kernel,
        grid=(bsz,),
        in_specs=[
            pl.BlockSpec((1, m, d), lambda b: (b, 0, 0)),
            pl.BlockSpec((1, d), lambda b: (0, 0)),
            pl.BlockSpec((d, 2 * XATTN_WIDTH), lambda b: (0, 0)),
        ],
        out_specs=[
            pl.BlockSpec((1, m, XATTN_WIDTH), lambda b: (b, 0, 0)),
            pl.BlockSpec((1, XATTN_WIDTH, m), lambda b: (b, 0, 0)),
        ],
        out_shape=[
            jax.ShapeDtypeStruct((bsz, m, XATTN_WIDTH), BF16),
            jax.ShapeDtypeStruct((bsz, XATTN_WIDTH, m), BF16),
        ],
        name="memkv",
        compiler_params=_params("arbitrary"),
    )(mem, gain, w)


INPROJ_TM = 512
INPROJ_CHUNK = 512


def _inproj_kernel(x_ref, g_ref, w_ref, xg_ref, qkv_ref):
    h = _rms(x_ref[...], g_ref[...]).astype(BF16)
    n_f32 = 2 * LRU_WIDTH // INPROJ_CHUNK
    for c in range(PROJ_COLS // INPROJ_CHUNK):
        lo = c * INPROJ_CHUNK
        y = jnp.dot(h, w_ref[:, lo:lo + INPROJ_CHUNK], preferred_element_type=F32)
        if c < n_f32:
            xg_ref[:, lo:lo + INPROJ_CHUNK] = y
        else:
            if c == n_f32:
                y = y * (MOBA_HEAD_DIM ** -0.5)
            o = lo - 2 * LRU_WIDTH
            qkv_ref[:, o:o + INPROJ_CHUNK] = y.astype(BF16)


def _inproj(x2, gain, w):
    t, d = x2.shape
    n_bf = PROJ_COLS - 2 * LRU_WIDTH
    return pl.pallas_call(
        _inproj_kernel,
        grid=(t // INPROJ_TM,),
        in_specs=[
            pl.BlockSpec((INPROJ_TM, d), lambda i: (i, 0)),
            pl.BlockSpec((1, d), lambda i: (0, 0)),
            pl.BlockSpec((d, PROJ_COLS), lambda i: (0, 0)),
        ],
        out_specs=[
            pl.BlockSpec((INPROJ_TM, 2 * LRU_WIDTH), lambda i: (i, 0)),
            pl.BlockSpec((INPROJ_TM, n_bf), lambda i: (i, 0)),
        ],
        out_shape=[
            jax.ShapeDtypeStruct((t, 2 * LRU_WIDTH), F32),
            jax.ShapeDtypeStruct((t, n_bf), BF16),
        ],
        name="inproj",
        compiler_params=_params("arbitrary"),
    )(x2, gain, w)


LRU_TM = 512
LRU_UNROLL = 4


def _lru_kernel(xg_ref, cw_ref, cb_ref, wg_ref, bg_ref, lam_ref, y_ref,
                xs_ref, a_ref, u_ref, h_ref, carry_ref):
    tm = LRU_TM
    w = LRU_WIDTH
    first = pl.program_id(1) == 0

    @pl.when(first)
    def _():
        xs_ref[0:V7X_SUBLANES, :] = jnp.zeros((V7X_SUBLANES, w), F32)
        carry_ref[...] = jnp.zeros((V7X_SUBLANES, w), F32)

    @pl.when(jnp.logical_not(first))
    def _():
        xs_ref[0:V7X_SUBLANES, :] = xs_ref[tm:tm + V7X_SUBLANES, :]

    xs_ref[V7X_SUBLANES:V7X_SUBLANES + tm, :] = xg_ref[:, :w]

    xc = cb_ref[...] + cw_ref[LRU_CONV - 1:LRU_CONV, :] * xs_ref[V7X_SUBLANES:V7X_SUBLANES + tm, :]
    for k in range(LRU_CONV - 1):
        off = V7X_SUBLANES - (LRU_CONV - 1) + k
        xc = xc + cw_ref[k:k + 1, :] * xs_ref[off:off + tm, :]

    gates = jnp.dot(xc.astype(BF16), wg_ref[...], preferred_element_type=F32) + bg_ref[...]
    r = jax.nn.sigmoid(gates[:, :w])
    i = jax.nn.sigmoid(gates[:, w:])
    z = -lam_ref[...]
    softplus = jnp.maximum(z, 0.0) + jnp.log1p(jnp.exp(-jnp.abs(z)))
    log_a = (-LRU_C) * r * softplus
    a = jnp.exp(log_a)
    th = jnp.tanh(log_a)
    mult = jnp.sqrt(-2.0 * th / (1.0 - th))
    a_ref[...] = a
    u_ref[...] = mult * i * xc

    sub = lax.broadcasted_iota(jnp.int32, (V7X_SUBLANES, w), 0)

    def body(g, carry):
        r0 = pl.multiple_of(g * V7X_SUBLANES, V7X_SUBLANES)
        av = a_ref[pl.ds(r0, V7X_SUBLANES), :]
        uv = u_ref[pl.ds(r0, V7X_SUBLANES), :]
        for d in (1, 2, 4):
            keep = sub >= d
            a_s = jnp.where(keep, pltpu.roll(av, d, 0), 1.0)
            u_s = jnp.where(keep, pltpu.roll(uv, d, 0), 0.0)
            uv = uv + av * u_s
            av = av * a_s
        hv = uv + av * carry
        h_ref[pl.ds(r0, V7X_SUBLANES), :] = hv
        return jnp.broadcast_to(hv[V7X_SUBLANES - 1:V7X_SUBLANES, :], (V7X_SUBLANES, w))

    carry_ref[...] = lax.fori_loop(0, tm // V7X_SUBLANES, body, carry_ref[...], unroll=LRU_UNROLL)
    y_ref[...] = (h_ref[...] * jax.nn.gelu(xg_ref[:, w:])).astype(BF16)


def _lru(xg, conv_w, conv_b, w_gates, b_gates, lam, bsz, s):
    t = xg.shape[0]
    w = LRU_WIDTH
    nt = s // LRU_TM
    return pl.pallas_call(
        _lru_kernel,
        grid=(bsz, nt),
        in_specs=[
            pl.BlockSpec((LRU_TM, 2 * w), lambda b, j: (b * nt + j, 0)),
            pl.BlockSpec((LRU_CONV, w), lambda b, j: (0, 0)),
            pl.BlockSpec((1, w), lambda b, j: (0, 0)),
            pl.BlockSpec((w, 2 * w), lambda b, j: (0, 0)),
            pl.BlockSpec((1, 2 * w), lambda b, j: (0, 0)),
            pl.BlockSpec((1, w), lambda b, j: (0, 0)),
        ],
        out_specs=pl.BlockSpec((LRU_TM, w), lambda b, j: (b * nt + j, 0)),
        out_shape=jax.ShapeDtypeStruct((t, w), BF16),
        scratch_shapes=[
            pltpu.VMEM((LRU_TM + 2 * V7X_SUBLANES, w), F32),
            pltpu.VMEM((LRU_TM, w), F32),
            pltpu.VMEM((LRU_TM, w), F32),
            pltpu.VMEM((LRU_TM, w), F32),
            pltpu.VMEM((V7X_SUBLANES, w), F32),
        ],
        name="lru",
        compiler_params=_params("arbitrary", "arbitrary"),
    )(xg, conv_w, conv_b, w_gates, b_gates, lam)


MOBA_PAIR = V7X_LANES // MOBA_HEAD_DIM


MOBA_AUX_POS, MOBA_AUX_BLK, MOBA_AUX_ONE = 0, 1, 2
MOBA_AUX_SEL = V7X_SUBLANES
MOBA_MASKED = -1e30
MOBA_DEN_ROWS = 2 * V7X_SUBLANES


def _moba_kernel(slopes_ref, q_ref, k_ref, v_ref, o_ref, vt_ref, kw_ref, qw_ref, *, seq):
    nb = seq // MOBA_BLOCK
    blk = MOBA_BLOCK
    wide = MOBA_PAIR * blk
    assert MOBA_PAIR == 2 and nb <= V7X_SUBLANES and blk & (blk - 1) == 0
    hp = pl.program_id(1)
    lane = lax.broadcasted_iota(jnp.int32, (1, V7X_LANES), 1)

    @pl.when((pl.program_id(0) == 0) & (hp == 0))
    def _():
        rk = lax.broadcasted_iota(jnp.int32, (seq, V7X_LANES), 0)
        lk = lax.broadcasted_iota(jnp.int32, (seq, V7X_LANES), 1)
        pos = jnp.bitwise_and(rk, blk - 1)
        blk_idx = lax.shift_right_logical(rk, blk.bit_length() - 1)
        aux = jnp.where(lk == MOBA_AUX_POS, pos.astype(F32),
                        jnp.where(lk == MOBA_AUX_BLK, (rk - pos).astype(F32),
                                  jnp.where((lk == MOBA_AUX_ONE) | (lk - MOBA_AUX_SEL == blk_idx), 1.0, 0.0)))
        kw_ref[:, V7X_LANES:] = aux.astype(BF16)
        ones_row = lax.broadcasted_iota(jnp.int32, (MOBA_DEN_ROWS, seq), 0) == 0
        vt_ref[V7X_LANES:, :] = jnp.where(ones_row, 1.0, 0.0).astype(BF16)

    vt_ref[:V7X_LANES, :] = v_ref[...].astype(F32).T.astype(BF16)
    kb = k_ref[...]
    kw_ref[:, :V7X_LANES] = kb
    kmean = jnp.mean(kb.astype(F32).reshape(nb, blk, V7X_LANES), axis=1)
    km_hi = kmean.astype(BF16)
    km_lo = (kmean - km_hi.astype(F32)).astype(BF16)

    rowid = lax.broadcasted_iota(jnp.int32, (V7X_SUBLANES, wide), 0)
    colid = lax.broadcasted_iota(jnp.int32, (V7X_SUBLANES, wide), 1)
    slope = jnp.where(colid < blk, slopes_ref[hp * MOBA_PAIR], slopes_ref[hp * MOBA_PAIR + 1])
    key_row = lax.broadcasted_iota(jnp.int32, (blk, wide), 0)
    qry_col = jnp.bitwise_and(lax.broadcasted_iota(jnp.int32, (blk, wide), 1), blk - 1)
    causal = key_row <= qry_col

    def query_operand(j):
        qj = q_ref[j * blk:(j + 1) * blk, :]
        zero = jnp.zeros_like(qj)
        q_heads = jnp.concatenate([jnp.where(lane < MOBA_HEAD_DIM, qj, zero),
                                   jnp.where(lane < MOBA_HEAD_DIM, zero, qj)], axis=0)
        q_rows = jnp.where(rowid <= MOBA_AUX_BLK, slope,
                           jnp.where(rowid == MOBA_AUX_ONE, -slope * (j * blk), 0.0))
        sel_rows = jnp.zeros((V7X_SUBLANES, wide), F32)
        if j > MOBA_TOPK:
            gate = _nt_dot(km_hi, q_heads) + _nt_dot(km_lo, q_heads)
            if nb < V7X_SUBLANES:
                gate = jnp.concatenate([gate, jnp.zeros((V7X_SUBLANES - nb, wide), F32)], axis=0)
            cnt = jnp.zeros((V7X_SUBLANES, wide), F32)
            for mth in range(j):
                gm = gate[mth:mth + 1, :]
                beats = (gm > gate) | ((gm == gate) & (rowid > mth))
                cnt = cnt + jnp.where(beats, 1.0, 0.0)
            sel_rows = jnp.where((rowid < j) & (cnt >= MOBA_TOPK), MOBA_MASKED, 0.0)
        q_aux = jnp.concatenate(
            [q_rows, sel_rows, jnp.zeros((V7X_LANES - 2 * V7X_SUBLANES, wide), F32)], axis=0).T.astype(BF16)
        qw_ref[j] = jnp.concatenate([q_heads, q_aux], axis=1)

    for j in range(nb):
        query_operand(j)

    def scores(j):
        q_wide = qw_ref[j]
        st_own = _nt_dot(kw_ref[j * blk:(j + 1) * blk, :], q_wide)
        st_own = jnp.where(causal, st_own, NEG_INF)
        m = jnp.max(st_own, axis=0, keepdims=True)
        st_past = None
        if j > 0:
            st_past = _nt_dot(kw_ref[0:j * blk, :], q_wide)
            m = jnp.maximum(m, jnp.max(st_past, axis=0, keepdims=True))
        return st_own, st_past, m

    def finish(j, st_own, st_past, m):
        p_own = jnp.exp((st_own - m).astype(BF16))
        acc = jnp.dot(vt_ref[:, j * blk:(j + 1) * blk], p_own, preferred_element_type=F32)
        if j > 0:
            p_past = jnp.exp((st_past - m).astype(BF16))
            acc = acc + jnp.dot(vt_ref[:, 0:j * blk], p_past, preferred_element_type=F32)
        ot = acc[:V7X_LANES, :] / acc[V7X_LANES:V7X_LANES + 1, :]
        o_t = jnp.concatenate([ot[:MOBA_HEAD_DIM, :blk], ot[MOBA_HEAD_DIM:, blk:]], axis=0)
        o_ref[j * blk:(j + 1) * blk, :] = o_t.T.astype(BF16)

    pending = scores(0)
    for j in range(nb):
        upcoming = scores(j + 1) if j + 1 < nb else None
        finish(j, *pending)
        pending = upcoming


def _moba(slopes, qkv, bsz, s):
    t = qkv.shape[0]
    n_pairs = MOBA_HEADS // MOBA_PAIR
    return pl.pallas_call(
        functools.partial(_moba_kernel, seq=s),
        grid_spec=pltpu.PrefetchScalarGridSpec(
            num_scalar_prefetch=1,
            grid=(bsz, n_pairs),
            in_specs=[
                pl.BlockSpec((s, V7X_LANES), lambda b, p, sl: (b, p)),
                pl.BlockSpec((s, V7X_LANES), lambda b, p, sl: (b, n_pairs + p)),
                pl.BlockSpec((s, V7X_LANES), lambda b, p, sl: (b, 2 * n_pairs + p)),
            ],
            out_specs=pl.BlockSpec((s, V7X_LANES), lambda b, p, sl: (b, p)),
            scratch_shapes=[
                pltpu.VMEM((V7X_LANES + MOBA_DEN_ROWS, s), BF16),
                pltpu.VMEM((s, 2 * V7X_LANES), BF16),
                pltpu.VMEM((s // MOBA_BLOCK, MOBA_PAIR * MOBA_BLOCK, 2 * V7X_LANES), BF16),
            ],
        ),
        out_shape=jax.ShapeDtypeStruct((t, MOBA_WIDTH), BF16),
        name="moba",
        compiler_params=_params("arbitrary", "arbitrary"),
    )(slopes, qkv, qkv, qkv)


XATTN_TQ = 512


def _xattn_kernel(q_ref, k_ref, vt_ref, o_ref):
    scale = XATTN_HEAD_DIM ** -0.5
    for h in range(XATTN_HEADS):
        lo = h * XATTN_HEAD_DIM
        st = _nt_dot(k_ref[0, :, lo:lo + XATTN_HEAD_DIM], q_ref[:, lo:lo + XATTN_HEAD_DIM]) * scale
        m = jnp.max(st, axis=0, keepdims=True)
        p = jnp.exp(st - m)
        l = jnp.sum(p, axis=0, keepdims=True)
        ot = jnp.dot(vt_ref[0, lo:lo + XATTN_HEAD_DIM, :], p.astype(BF16), preferred_element_type=F32)
        o_ref[:, lo:lo + XATTN_HEAD_DIM] = (ot / l).T.astype(BF16)


def _xattn(qkv, mk, mvt, bsz, s):
    t = qkv.shape[0]
    nt = s // XATTN_TQ
    m = mk.shape[1]
    qx_block = 3 * MOBA_WIDTH // XATTN_WIDTH
    return pl.pallas_call(
        _xattn_kernel,
        grid=(bsz, nt),
        in_specs=[
            pl.BlockSpec((XATTN_TQ, XATTN_WIDTH), lambda b, j: (b * nt + j, qx_block)),
            pl.BlockSpec((1, m, XATTN_WIDTH), lambda b, j: (b, 0, 0)),
            pl.BlockSpec((1, XATTN_WIDTH, m), lambda b, j: (b, 0, 0)),
        ],
        out_specs=pl.BlockSpec((XATTN_TQ, XATTN_WIDTH), lambda b, j: (b * nt + j, 0)),
        out_shape=jax.ShapeDtypeStruct((t, XATTN_WIDTH), BF16),
        name="xattn",
        compiler_params=_params("arbitrary", "arbitrary"),
    )(qkv, mk, mvt)


MERGE_TM = 512


def _merge_kernel(x_ref, g_ref, ya_ref, yb_ref, yc_ref, wg_ref, wb_ref, wo_ref, o_ref):
    x = x_ref[...]
    h = _rms(x, g_ref[...]).astype(BF16)
    merged = None
    for n, y_ref in enumerate((ya_ref, yb_ref, yc_ref)):
        logits = jnp.dot(h, wg_ref[:, n * D_MODEL:(n + 1) * D_MODEL], preferred_element_type=F32)
        branch = jnp.dot(y_ref[...], wb_ref[n], preferred_element_type=F32)
        term = jax.nn.sigmoid(logits) * branch
        merged = term if merged is None else merged + term
    o_ref[...] = x + jnp.dot(merged.astype(BF16), wo_ref[...], preferred_element_type=F32)


def _merge(x2, gain, ya, yb, yc, w_gate, w_branch, w_out):
    t, d = x2.shape
    w = LRU_WIDTH
    return pl.pallas_call(
        _merge_kernel,
        grid=(t // MERGE_TM,),
        in_specs=[
            pl.BlockSpec((MERGE_TM, d), lambda i: (i, 0)),
            pl.BlockSpec((1, d), lambda i: (0, 0)),
            pl.BlockSpec((MERGE_TM, w), lambda i: (i, 0)),
            pl.BlockSpec((MERGE_TM, w), lambda i: (i, 0)),
            pl.BlockSpec((MERGE_TM, w), lambda i: (i, 0)),
            pl.BlockSpec((d, N_BRANCH * d), lambda i: (0, 0)),
            pl.BlockSpec((N_BRANCH, w, d), lambda i: (0, 0, 0)),
            pl.BlockSpec((d, d), lambda i: (0, 0)),
        ],
        out_specs=pl.BlockSpec((MERGE_TM, d), lambda i: (i, 0)),
        out_shape=jax.ShapeDtypeStruct((t, d), F32),
        name="merge",
        compiler_params=_params("arbitrary"),
    )(x2, gain, ya, yb, yc, w_gate, w_branch, w_out)


FFN_TM = 512
FFN_GROUP = 1024
FFN_SUB = 256


def _ffn_kernel(x_ref, g_ref, wg_ref, wu_ref, cw_ref, cb_ref, wd_ref, fg_ref, o_ref, tail_ref,
                *, tiles_per_seq, final_norm):
    tm = FFN_TM
    i = pl.program_id(0)

    @pl.when(i == 0)
    def _():
        tail_ref[...] = jnp.zeros_like(tail_ref)

    x = x_ref[...]
    h = _rms(x, g_ref[...]).astype(BF16)
    sub = lax.broadcasted_iota(jnp.int32, (V7X_SUBLANES, FFN_SUB), 0)
    mid_seq = jnp.broadcast_to(i % tiles_per_seq, (V7X_SUBLANES, FFN_SUB)) != 0
    y = x
    for grp in range(D_FF // FFN_GROUP):
        acts = []
        for k in range(FFN_GROUP // FFN_SUB):
            lo = grp * FFN_GROUP + k * FFN_SUB
            g = jnp.dot(h, wg_ref[:, lo:lo + FFN_SUB], preferred_element_type=F32)
            u = jnp.dot(h, wu_ref[:, lo:lo + FFN_SUB], preferred_element_type=F32)
            prev = jnp.where(mid_seq, tail_ref[:, lo:lo + FFN_SUB], 0.0)
            tail_ref[:, lo:lo + FFN_SUB] = g[tm - V7X_SUBLANES:tm, :]
            conv = cb_ref[:, lo:lo + FFN_SUB] + cw_ref[FFN_CONV - 1:FFN_CONV, lo:lo + FFN_SUB] * g
            for shift in range(1, FFN_CONV):
                rolled = pltpu.roll(g, shift, 0)
                head = jnp.where(sub < shift, pltpu.roll(prev, shift, 0), rolled[0:V7X_SUBLANES, :])
                shifted = jnp.concatenate([head, rolled[V7X_SUBLANES:, :]], axis=0)
                tap = FFN_CONV - 1 - shift
                conv = conv + cw_ref[tap:tap + 1, lo:lo + FFN_SUB] * shifted
            acts.append((jax.nn.gelu(conv) * u).astype(BF16))
        act = jnp.concatenate(acts, axis=1)
        y = y + jnp.dot(act, wd_ref[grp * FFN_GROUP:(grp + 1) * FFN_GROUP, :], preferred_element_type=F32)
    if final_norm:
        y = _rms(y, fg_ref[...])
    o_ref[...] = y


def _ffn(x2, gain, w_gate, w_up, conv_w, conv_b, w_down, final_gain, s, final_norm):
    t, d = x2.shape
    kern = functools.partial(_ffn_kernel, tiles_per_seq=s // FFN_TM, final_norm=final_norm)
    resident = lambda shape: pl.BlockSpec(shape, lambda i: (0,) * len(shape), pipeline_mode=pl.Buffered(1))
    return pl.pallas_call(
        kern,
        grid=(t // FFN_TM,),
        in_specs=[
            pl.BlockSpec((FFN_TM, d), lambda i: (i, 0)),
            resident((1, d)),
            resident((d, D_FF)),
            resident((d, D_FF)),
            resident((FFN_CONV, D_FF)),
            resident((1, D_FF)),
            resident((D_FF, d)),
            resident((1, d)),
        ],
        out_specs=pl.BlockSpec((FFN_TM, d), lambda i: (i, 0)),
        out_shape=jax.ShapeDtypeStruct((t, d), F32),
        scratch_shapes=[pltpu.VMEM((V7X_SUBLANES, D_FF), F32)],
        name="ffn_final" if final_norm else "ffn",
        compiler_params=_params("arbitrary"),
    )(x2, gain, w_gate, w_up, conv_w, conv_b, w_down, final_gain)


def _block_diag(w):
    nh, hd, _ = w.shape
    eye = jnp.eye(nh, dtype=w.dtype)
    return (eye[:, None, :, None] * w[:, :, None, :]).reshape(nh * hd, nh * hd)


def kernel(x, mem, mix_norm_gain, w_in, lru_conv_w, lru_conv_b, lru_w_a, lru_b_a, lru_w_x, lru_b_x, lru_lambda,
           mem_norm_gain, w_mem_kv, w_branch, w_out, ffn_norm_gain, w_ffn_gate, w_ffn_up, ffn_conv_w, ffn_conv_b,
           w_ffn_down, final_norm_gain):
    bsz, s, d = x.shape
    depth = w_in.shape[0]
    x2 = x.reshape(bsz * s, d)
    slopes = jnp.exp2(-8.0 * jnp.arange(1, MOBA_HEADS + 1, dtype=F32) / MOBA_HEADS)
    row = lambda v: v.reshape(1, -1)
    for l in range(depth):
        w_proj = w_in[l, :, :PROJ_COLS].astype(BF16)
        w_glog = w_in[l, :, PROJ_COLS:].astype(BF16)
        w_gates = jnp.concatenate([_block_diag(lru_w_a[l]), _block_diag(lru_w_x[l])], axis=1).astype(BF16)
        b_gates = jnp.concatenate([lru_b_a[l].reshape(-1), lru_b_x[l].reshape(-1)]).reshape(1, -1)

        mk, mvt = _memkv(mem, row(mem_norm_gain[l]), w_mem_kv[l].astype(BF16))
        xg, qkv = _inproj(x2, row(mix_norm_gain[l]), w_proj)
        ya = _lru(xg, lru_conv_w[l], row(lru_conv_b[l]), w_gates, b_gates, row(lru_lambda[l]), bsz, s)
        yb = _moba(slopes, qkv, bsz, s)
        yc = _xattn(qkv, mk, mvt, bsz, s)
        x2 = _merge(x2, row(mix_norm_gain[l]), ya, yb, yc, w_glog, w_branch[l].astype(BF16), w_out[l].astype(BF16))
        x2 = _ffn(x2, row(ffn_norm_gain[l]), w_ffn_gate[l].astype(BF16), w_ffn_up[l].astype(BF16),
                  ffn_conv_w[l], row(ffn_conv_b[l]), w_ffn_down[l].astype(BF16), row(final_norm_gain),
                  s, final_norm=(l == depth - 1))
    return x2.reshape(bsz, s, d)
```

```python
import functools

import jax
import jax.numpy as jnp
from jax import lax
from jax.experimental import pallas as pl
from jax.experimental.pallas import tpu as pltpu

F32 = jnp.float32
BF16 = jnp.bfloat16

D_MODEL = 1024
LRU_WIDTH = 512
LRU_HEADS = 8
LRU_HEAD_DIM = LRU_WIDTH // LRU_HEADS
LRU_CONV = 4
LRU_C = 8.0
MOBA_HEADS = 8
MOBA_HEAD_DIM = 64
MOBA_WIDTH = MOBA_HEADS * MOBA_HEAD_DIM
MOBA_BLOCK = 256
MOBA_TOPK = 3
XATTN_HEADS = 4
XATTN_HEAD_DIM = 128
XATTN_WIDTH = XATTN_HEADS * XATTN_HEAD_DIM
N_BRANCH = 3
D_FF = 3 * D_MODEL
FFN_CONV = 3
NORM_EPS = 1e-6

V7X_SUBLANES = 8
V7X_LANES = 128
VMEM_LIMIT_BYTES = 56 * 1024 * 1024

PROJ_COLS = 2 * LRU_WIDTH + 3 * MOBA_WIDTH + XATTN_WIDTH
NEG_INF = float("-inf")


def _params(*semantics):
    return pltpu.CompilerParams(dimension_semantics=semantics, vmem_limit_bytes=VMEM_LIMIT_BYTES)


def _rms(x, g):
    return x * lax.rsqrt(jnp.mean(x * x, axis=-1, keepdims=True) + NORM_EPS) * g


def _nt_dot(a, b):
    return lax.dot_general(a, b, (((1,), (1,)), ((), ())), preferred_element_type=F32)


XATTN_DEN_ROWS = 2 * V7X_SUBLANES
XATTN_VT_ROWS = XATTN_HEAD_DIM + XATTN_DEN_ROWS

def _memkv_kernel(mem_ref, g_ref, w_ref, k_ref, vt_ref):
    h = _rms(mem_ref[0], g_ref[...]).astype(BF16)
    k_ref[0] = jnp.dot(h, w_ref[:, :XATTN_WIDTH], preferred_element_type=F32).astype(BF16)
    v = jnp.dot(h, w_ref[:, XATTN_WIDTH:], preferred_element_type=F32)
    ones_row = lax.broadcasted_iota(jnp.int32, (XATTN_DEN_ROWS, v.shape[0]), 0) == 0
    for hd in range(XATTN_HEADS):
        lo = hd * XATTN_VT_ROWS
        vt_ref[0, lo:lo + XATTN_HEAD_DIM, :] = v[:, hd * XATTN_HEAD_DIM:(hd + 1) * XATTN_HEAD_DIM].T.astype(BF16)
        vt_ref[0, lo + XATTN_HEAD_DIM:lo + XATTN_VT_ROWS, :] = jnp.where(ones_row, 1.0, 0.0).astype(BF16)


def _memkv(mem, gain, w):
    bsz, m, d = mem.shape
    return pl.pallas_call(
        _memkv_kernel,
        grid=(bsz,),
        in_specs=[
            pl.BlockSpec((1, m, d), lambda b: (b, 0, 0)),
            pl.BlockSpec((1, d), lambda b: (0, 0)),
            pl.BlockSpec((d, 2 * XATTN_WIDTH), lambda b: (0, 0)),
        ],
        out_specs=[
            pl.BlockSpec((1, m, XATTN_WIDTH), lambda b: (b, 0, 0)),
            pl.BlockSpec((1, XATTN_HEADS * XATTN_VT_ROWS, m), lambda b: (b, 0, 0)),
        ],
        out_shape=[
            jax.ShapeDtypeStruct((bsz, m, XATTN_WIDTH), BF16),
            jax.ShapeDtypeStruct((bsz, XATTN_HEADS * XATTN_VT_ROWS, m), BF16),
        ],
        name="memkv",
        compiler_params=_params("arbitrary"),
    )(mem, gain, w)


INPROJ_TM = 512
INPROJ_CHUNK = 512


def _inproj_kernel(x_ref, g_ref, w_ref, xg_ref, qkv_ref):
    h = _rms(x_ref[...], g_ref[...]).astype(BF16)
    n_f32 = 2 * LRU_WIDTH // INPROJ_CHUNK
    for c in range(PROJ_COLS // INPROJ_CHUNK):
        lo = c * INPROJ_CHUNK
        y = jnp.dot(h, w_ref[:, lo:lo + INPROJ_CHUNK], preferred_element_type=F32)
        if c < n_f32:
            xg_ref[:, lo:lo + INPROJ_CHUNK] = y
        else:
            if c == n_f32:
                y = y * (MOBA_HEAD_DIM ** -0.5)
            o = lo - 2 * LRU_WIDTH
            qkv_ref[:, o:o + INPROJ_CHUNK] = y.astype(BF16)


def _inproj(x2, gain, w):
    t, d = x2.shape
    n_bf = PROJ_COLS - 2 * LRU_WIDTH
    return pl.pallas_call(
        _inproj_kernel,
        grid=(t // INPROJ_TM,),
        in_specs=[
            pl.BlockSpec((INPROJ_TM, d), lambda i: (i, 0)),
            pl.BlockSpec((1, d), lambda i: (0, 0)),
            pl.BlockSpec((d, PROJ_COLS), lambda i: (0, 0)),
        ],
        out_specs=[
            pl.BlockSpec((INPROJ_TM, 2 * LRU_WIDTH), lambda i: (i, 0)),
            pl.BlockSpec((INPROJ_TM, n_bf), lambda i: (i, 0)),
        ],
        out_shape=[
            jax.ShapeDtypeStruct((t, 2 * LRU_WIDTH), F32),
            jax.ShapeDtypeStruct((t, n_bf), BF16),
        ],
        name="inproj",
        compiler_params=_params("arbitrary"),
    )(x2, gain, w)


LRU_TM = 512
LRU_UNROLL = 4


def _lru_kernel(xg_ref, cw_ref, cb_ref, wg_ref, bg_ref, lam_ref, y_ref,
                xs_ref, a_ref, u_ref, h_ref, carry_ref):
    tm = LRU_TM
    w = LRU_WIDTH
    first = pl.program_id(1) == 0

    @pl.when(first)
    def _():
        xs_ref[0:V7X_SUBLANES, :] = jnp.zeros((V7X_SUBLANES, w), F32)
        carry_ref[...] = jnp.zeros((V7X_SUBLANES, w), F32)

    @pl.when(jnp.logical_not(first))
    def _():
        xs_ref[0:V7X_SUBLANES, :] = xs_ref[tm:tm + V7X_SUBLANES, :]

    xs_ref[V7X_SUBLANES:V7X_SUBLANES + tm, :] = xg_ref[:, :w]

    xc = cb_ref[...] + cw_ref[LRU_CONV - 1:LRU_CONV, :] * xs_ref[V7X_SUBLANES:V7X_SUBLANES + tm, :]
    for k in range(LRU_CONV - 1):
        off = V7X_SUBLANES - (LRU_CONV - 1) + k
        xc = xc + cw_ref[k:k + 1, :] * xs_ref[off:off + tm, :]

    gates = jnp.dot(xc.astype(BF16), wg_ref[...], preferred_element_type=F32) + bg_ref[...]
    r = jax.nn.sigmoid(gates[:, :w])
    i = jax.nn.sigmoid(gates[:, w:])
    z = -lam_ref[...]
    softplus = jnp.maximum(z, 0.0) + jnp.log1p(jnp.exp(-jnp.abs(z)))
    log_a = (-LRU_C) * r * softplus
    a = jnp.exp(log_a)
    th = jnp.tanh(log_a)
    mult = jnp.sqrt(-2.0 * th / (1.0 - th))
    a_ref[...] = a
    u_ref[...] = mult * i * xc

    sub = lax.broadcasted_iota(jnp.int32, (V7X_SUBLANES, w), 0)

    def body(g, carry):
        r0 = pl.multiple_of(g * V7X_SUBLANES, V7X_SUBLANES)
        av = a_ref[pl.ds(r0, V7X_SUBLANES), :]
        uv = u_ref[pl.ds(r0, V7X_SUBLANES), :]
        for d in (1, 2, 4):
            keep = sub >= d
            a_s = jnp.where(keep, pltpu.roll(av, d, 0), 1.0)
            u_s = jnp.where(keep, pltpu.roll(uv, d, 0), 0.0)
            uv = uv + av * u_s
            av = av * a_s
        hv = uv + av * carry
        h_ref[pl.ds(r0, V7X_SUBLANES), :] = hv
        return jnp.broadcast_to(hv[V7X_SUBLANES - 1:V7X_SUBLANES, :], (V7X_SUBLANES, w))

    carry_ref[...] = lax.fori_loop(0, tm // V7X_SUBLANES, body, carry_ref[...], unroll=LRU_UNROLL)
    y_ref[...] = (h_ref[...] * jax.nn.gelu(xg_ref[:, w:])).astype(BF16)


def _lru(xg, conv_w, conv_b, w_gates, b_gates, lam, bsz, s):
    t = xg.shape[0]
    w = LRU_WIDTH
    nt = s // LRU_TM
    return pl.pallas_call(
        _lru_kernel,
        grid=(bsz, nt),
        in_specs=[
            pl.BlockSpec((LRU_TM, 2 * w), lambda b, j: (b * nt + j, 0)),
            pl.BlockSpec((LRU_CONV, w), lambda b, j: (0, 0)),
            pl.BlockSpec((1, w), lambda b, j: (0, 0)),
            pl.BlockSpec((w, 2 * w), lambda b, j: (0, 0)),
            pl.BlockSpec((1, 2 * w), lambda b, j: (0, 0)),
            pl.BlockSpec((1, w), lambda b, j: (0, 0)),
        ],
        out_specs=pl.BlockSpec((LRU_TM, w), lambda b, j: (b * nt + j, 0)),
        out_shape=jax.ShapeDtypeStruct((t, w), BF16),
        scratch_shapes=[
            pltpu.VMEM((LRU_TM + 2 * V7X_SUBLANES, w), F32),
            pltpu.VMEM((LRU_TM, w), F32),
            pltpu.VMEM((LRU_TM, w), F32),
            pltpu.VMEM((LRU_TM, w), F32),
            pltpu.VMEM((V7X_SUBLANES, w), F32),
        ],
        name="lru",
        compiler_params=_params("arbitrary", "arbitrary"),
    )(xg, conv_w, conv_b, w_gates, b_gates, lam)


MOBA_PAIR = V7X_LANES // MOBA_HEAD_DIM


MOBA_AUX_POS, MOBA_AUX_BLK, MOBA_AUX_ONE = 0, 1, 2
MOBA_AUX_SEL = V7X_SUBLANES
MOBA_MASKED = -1e30
MOBA_DEN_ROWS = 2 * V7X_SUBLANES
MOBA_BOUND_SLACK = 1.02
MOBA_MIN_DEN = 1e-17


def _moba_kernel(slopes_ref, q_ref, k_ref, v_ref, o_ref, vt_ref, kw_ref, qw_ref, *, seq):
    nb = seq // MOBA_BLOCK
    blk = MOBA_BLOCK
    wide = MOBA_PAIR * blk
    assert MOBA_PAIR == 2 and nb <= V7X_SUBLANES and blk & (blk - 1) == 0
    hp = pl.program_id(1)
    lane = lax.broadcasted_iota(jnp.int32, (1, V7X_LANES), 1)

    @pl.when((pl.program_id(0) == 0) & (hp == 0))
    def _():
        rk = lax.broadcasted_iota(jnp.int32, (seq, V7X_LANES), 0)
        lk = lax.broadcasted_iota(jnp.int32, (seq, V7X_LANES), 1)
        pos = jnp.bitwise_and(rk, blk - 1)
        blk_idx = lax.shift_right_logical(rk, blk.bit_length() - 1)
        aux = jnp.where(lk == MOBA_AUX_POS, pos.astype(F32),
                        jnp.where(lk == MOBA_AUX_BLK, (rk - pos).astype(F32),
                                  jnp.where((lk == MOBA_AUX_ONE) | (lk - MOBA_AUX_SEL == blk_idx), 1.0, 0.0)))
        kw_ref[:, V7X_LANES:] = aux.astype(BF16)
        ones_row = lax.broadcasted_iota(jnp.int32, (MOBA_DEN_ROWS, seq), 0) == 0
        vt_ref[V7X_LANES:, :] = jnp.where(ones_row, 1.0, 0.0).astype(BF16)

    vt_ref[:V7X_LANES, :] = v_ref[...].astype(F32).T.astype(BF16)
    kb = k_ref[...]
    kw_ref[:, :V7X_LANES] = kb
    kf = kb.astype(F32)
    kmean = jnp.mean(kf.reshape(nb, blk, V7X_LANES), axis=1)
    km_hi = kmean.astype(BF16)
    km_lo = (kmean - km_hi.astype(F32)).astype(BF16)

    head_sum = (lax.shift_right_logical(lax.broadcasted_iota(jnp.int32, (V7X_LANES, V7X_LANES), 0),
                                        MOBA_HEAD_DIM.bit_length() - 1)
                == lax.broadcasted_iota(jnp.int32, (V7X_LANES, V7X_LANES), 1))
    kn2 = jnp.dot((kf * kf).astype(BF16), jnp.where(head_sum, 1.0, 0.0).astype(BF16),
                  preferred_element_type=F32)
    kn2_blk = jnp.max(kn2.reshape(nb, blk, V7X_LANES), axis=1)
    kn_upto, running = [], None
    for j in range(nb):
        row = kn2_blk[j:j + 1, :]
        running = row if running is None else jnp.maximum(running, row)
        kn_upto.append(jnp.sqrt(running))

    rowid = lax.broadcasted_iota(jnp.int32, (V7X_SUBLANES, wide), 0)
    colid = lax.broadcasted_iota(jnp.int32, (V7X_SUBLANES, wide), 1)
    first_head = colid < blk
    slope = jnp.where(first_head, slopes_ref[hp * MOBA_PAIR], slopes_ref[hp * MOBA_PAIR + 1])
    q_offset = jnp.bitwise_and(colid, blk - 1).astype(F32)
    key_row = lax.broadcasted_iota(jnp.int32, (blk, wide), 0)
    qry_col = jnp.bitwise_and(lax.broadcasted_iota(jnp.int32, (blk, wide), 1), blk - 1)
    causal = key_row <= qry_col
    ones_lhs = jnp.ones((V7X_SUBLANES, V7X_LANES), BF16)

    def query_operand(j):
        qj = q_ref[j * blk:(j + 1) * blk, :]
        zero = jnp.zeros_like(qj)
        q_heads = jnp.concatenate([jnp.where(lane < MOBA_HEAD_DIM, qj, zero),
                                   jnp.where(lane < MOBA_HEAD_DIM, zero, qj)], axis=0)
        q_norm = jnp.sqrt(_nt_dot(ones_lhs, jnp.square(q_heads.astype(F32)).astype(BF16)))
        k_norm = jnp.where(first_head,
                           jnp.sum(jnp.where(lane == 0, kn_upto[j], 0.0), axis=1, keepdims=True),
                           jnp.sum(jnp.where(lane == 1, kn_upto[j], 0.0), axis=1, keepdims=True))
        bound = MOBA_BOUND_SLACK * q_norm * k_norm + slope * q_offset
        q_rows = jnp.where(rowid <= MOBA_AUX_BLK, slope,
                           jnp.where(rowid == MOBA_AUX_ONE, -slope * (j * blk) - bound, 0.0))
        sel_rows = jnp.zeros((V7X_SUBLANES, wide), F32)
        if j > MOBA_TOPK:
            gate = _nt_dot(km_hi, q_heads) + _nt_dot(km_lo, q_heads)
            if nb < V7X_SUBLANES:
                gate = jnp.concatenate([gate, jnp.zeros((V7X_SUBLANES - nb, wide), F32)], axis=0)
            cnt = jnp.zeros((V7X_SUBLANES, wide), F32)
            for mth in range(j):
                gm = gate[mth:mth + 1, :]
                beats = (gm > gate) | ((gm == gate) & (rowid > mth))
                cnt = cnt + jnp.where(beats, 1.0, 0.0)
            sel_rows = jnp.where((rowid < j) & (cnt >= MOBA_TOPK), MOBA_MASKED, 0.0)
        q_aux = jnp.concatenate(
            [q_rows, sel_rows, jnp.zeros((V7X_LANES - 2 * V7X_SUBLANES, wide), F32)], axis=0).T.astype(BF16)
        qw_ref[j] = jnp.concatenate([q_heads, q_aux], axis=1)

    def shifted_scores(j):
        q_wide = qw_ref[j]
        st_own = _nt_dot(kw_ref[j * blk:(j + 1) * blk, :], q_wide)
        st_own = jnp.where(causal, st_own, NEG_INF)
        st_past = _nt_dot(kw_ref[0:j * blk, :], q_wide) if j > 0 else None
        return st_own, st_past

    def attend(j, st_own, st_past, exponent):
        acc = jnp.dot(vt_ref[:, j * blk:(j + 1) * blk], exponent(st_own), preferred_element_type=F32)
        if j > 0:
            acc = acc + jnp.dot(vt_ref[:, 0:j * blk], exponent(st_past), preferred_element_type=F32)
        den = acc[V7X_LANES:V7X_LANES + 1, :]
        ot = acc[:V7X_LANES, :] / den
        o_t = jnp.concatenate([ot[:MOBA_HEAD_DIM, :blk], ot[MOBA_HEAD_DIM:, blk:]], axis=0)
        o_ref[j * blk:(j + 1) * blk, :] = o_t.T.astype(BF16)
        return den

    den_min = None
    for j in range(nb):
        query_operand(j)
    for j in range(nb):
        den = attend(j, *shifted_scores(j), lambda st: jnp.exp(st).astype(BF16))
        den_min = den if den_min is None else jnp.minimum(den_min, den)

    @pl.when(jnp.logical_not(jnp.min(den_min) >= MOBA_MIN_DEN))
    def _():
        for j in range(nb):
            st_own, st_past = shifted_scores(j)
            m = jnp.max(st_own, axis=0, keepdims=True)
            if j > 0:
                m = jnp.maximum(m, jnp.max(st_past, axis=0, keepdims=True))
            attend(j, st_own, st_past, lambda st, m=m: jnp.exp(st - m).astype(BF16))


def _moba(slopes, qkv, bsz, s):
    t = qkv.shape[0]
    n_pairs = MOBA_HEADS // MOBA_PAIR
    return pl.pallas_call(
        functools.partial(_moba_kernel, seq=s),
        grid_spec=pltpu.PrefetchScalarGridSpec(
            num_scalar_prefetch=1,
            grid=(bsz, n_pairs),
            in_specs=[
                pl.BlockSpec((s, V7X_LANES), lambda b, p, sl: (b, p)),
                pl.BlockSpec((s, V7X_LANES), lambda b, p, sl: (b, n_pairs + p)),
                pl.BlockSpec((s, V7X_LANES), lambda b, p, sl: (b, 2 * n_pairs + p)),
            ],
            out_specs=pl.BlockSpec((s, V7X_LANES), lambda b, p, sl: (b, p)),
            scratch_shapes=[
                pltpu.VMEM((V7X_LANES + MOBA_DEN_ROWS, s), BF16),
                pltpu.VMEM((s, 2 * V7X_LANES), BF16),
                pltpu.VMEM((s // MOBA_BLOCK, MOBA_PAIR * MOBA_BLOCK, 2 * V7X_LANES), BF16),
            ],
        ),
        out_shape=jax.ShapeDtypeStruct((t, MOBA_WIDTH), BF16),
        name="moba",
        compiler_params=_params("arbitrary", "arbitrary"),
    )(slopes, qkv, qkv, qkv)


XATTN_TQ = 512


def _xattn_kernel(q_ref, k_ref, vt_ref, o_ref):
    scale = XATTN_HEAD_DIM ** -0.5
    heads = [slice(h * XATTN_HEAD_DIM, (h + 1) * XATTN_HEAD_DIM) for h in range(XATTN_HEADS)]
    scores = [_nt_dot(k_ref[0, :, hs], q_ref[:, hs]) * scale for hs in heads]
    for h, hs in enumerate(heads):
        st = scores[h]
        m = jnp.max(st, axis=0, keepdims=True)
        p = jnp.exp((st - m).astype(BF16))
        acc = jnp.dot(vt_ref[0, h * XATTN_VT_ROWS:(h + 1) * XATTN_VT_ROWS, :], p, preferred_element_type=F32)
        ot = acc[:XATTN_HEAD_DIM, :] / acc[XATTN_HEAD_DIM:XATTN_HEAD_DIM + 1, :]
        o_ref[:, hs] = ot.T.astype(BF16)


def _xattn(qkv, mk, mvt, bsz, s):
    t = qkv.shape[0]
    nt = s // XATTN_TQ
    m = mk.shape[1]
    qx_block = 3 * MOBA_WIDTH // XATTN_WIDTH
    return pl.pallas_call(
        _xattn_kernel,
        grid=(bsz, nt),
        in_specs=[
            pl.BlockSpec((XATTN_TQ, XATTN_WIDTH), lambda b, j: (b * nt + j, qx_block)),
            pl.BlockSpec((1, m, XATTN_WIDTH), lambda b, j: (b, 0, 0)),
            pl.BlockSpec((1, XATTN_HEADS * XATTN_VT_ROWS, m), lambda b, j: (b, 0, 0)),
        ],
        out_specs=pl.BlockSpec((XATTN_TQ, XATTN_WIDTH), lambda b, j: (b * nt + j, 0)),
        out_shape=jax.ShapeDtypeStruct((t, XATTN_WIDTH), BF16),
        name="xattn",
        compiler_params=_params("arbitrary", "arbitrary"),
    )(qkv, mk, mvt)


MERGE_TM = 512


def _merge_kernel(x_ref, g_ref, ya_ref, yb_ref, yc_ref, wg_ref, wb_ref, wo_ref, o_ref):
    x = x_ref[...]
    h = _rms(x, g_ref[...]).astype(BF16)
    merged = None
    for n, y_ref in enumerate((ya_ref, yb_ref, yc_ref)):
        logits = jnp.dot(h, wg_ref[:, n * D_MODEL:(n + 1) * D_MODEL], preferred_element_type=F32)
        branch = jnp.dot(y_ref[...], wb_ref[n], preferred_element_type=F32)
        term = jax.nn.sigmoid(logits) * branch
        merged = term if merged is None else merged + term
    o_ref[...] = x + jnp.dot(merged.astype(BF16), wo_ref[...], preferred_element_type=F32)


def _merge(x2, gain, ya, yb, yc, w_gate, w_branch, w_out):
    t, d = x2.shape
    w = LRU_WIDTH
    return pl.pallas_call(
        _merge_kernel,
        grid=(t // MERGE_TM,),
        in_specs=[
            pl.BlockSpec((MERGE_TM, d), lambda i: (i, 0)),
            pl.BlockSpec((1, d), lambda i: (0, 0)),
            pl.BlockSpec((MERGE_TM, w), lambda i: (i, 0)),
            pl.BlockSpec((MERGE_TM, w), lambda i: (i, 0)),
            pl.BlockSpec((MERGE_TM, w), lambda i: (i, 0)),
            pl.BlockSpec((d, N_BRANCH * d), lambda i: (0, 0)),
            pl.BlockSpec((N_BRANCH, w, d), lambda i: (0, 0, 0)),
            pl.BlockSpec((d, d), lambda i: (0, 0)),
        ],
        out_specs=pl.BlockSpec((MERGE_TM, d), lambda i: (i, 0)),
        out_shape=jax.ShapeDtypeStruct((t, d), F32),
        name="merge",
        compiler_params=_params("arbitrary"),
    )(x2, gain, ya, yb, yc, w_gate, w_branch, w_out)


FFN_TM = 512
FFN_GROUP = 1024
FFN_SUB = 256


def _ffn_kernel(x_ref, g_ref, wg_ref, wu_ref, cw_ref, cb_ref, wd_ref, fg_ref, o_ref, tail_ref,
                *, tiles_per_seq, final_norm):
    tm = FFN_TM
    i = pl.program_id(0)

    @pl.when(i == 0)
    def _():
        tail_ref[...] = jnp.zeros_like(tail_ref)

    x = x_ref[...]
    h = _rms(x, g_ref[...]).astype(BF16)
    sub = lax.broadcasted_iota(jnp.int32, (V7X_SUBLANES, FFN_SUB), 0)
    mid_seq = jnp.broadcast_to(i % tiles_per_seq, (V7X_SUBLANES, FFN_SUB)) != 0
    y = x
    for grp in range(D_FF // FFN_GROUP):
        acts = []
        for k in range(FFN_GROUP // FFN_SUB):
            lo = grp * FFN_GROUP + k * FFN_SUB
            g = jnp.dot(h, wg_ref[:, lo:lo + FFN_SUB], preferred_element_type=F32)
            u = jnp.dot(h, wu_ref[:, lo:lo + FFN_SUB], preferred_element_type=F32)
            prev = jnp.where(mid_seq, tail_ref[:, lo:lo + FFN_SUB], 0.0)
            tail_ref[:, lo:lo + FFN_SUB] = g[tm - V7X_SUBLANES:tm, :]
            conv = cb_ref[:, lo:lo + FFN_SUB] + cw_ref[FFN_CONV - 1:FFN_CONV, lo:lo + FFN_SUB] * g
            for shift in range(1, FFN_CONV):
                rolled = pltpu.roll(g, shift, 0)
                head = jnp.where(sub < shift, pltpu.roll(prev, shift, 0), rolled[0:V7X_SUBLANES, :])
                shifted = jnp.concatenate([head, rolled[V7X_SUBLANES:, :]], axis=0)
                tap = FFN_CONV - 1 - shift
                conv = conv + cw_ref[tap:tap + 1, lo:lo + FFN_SUB] * shifted
            acts.append((jax.nn.gelu(conv) * u).astype(BF16))
        act = jnp.concatenate(acts, axis=1)
        y = y + jnp.dot(act, wd_ref[grp * FFN_GROUP:(grp + 1) * FFN_GROUP, :], preferred_element_type=F32)
    if final_norm:
        y = _rms(y, fg_ref[...])
    o_ref[...] = y


def _ffn(x2, gain, w_gate, w_up, conv_w, conv_b, w_down, final_gain, s, final_norm):
    t, d = x2.shape
    kern = functools.partial(_ffn_kernel, tiles_per_seq=s // FFN_TM, final_norm=final_norm)
    resident = lambda shape: pl.BlockSpec(shape, lambda i: (0,) * len(shape), pipeline_mode=pl.Buffered(1))
    return pl.pallas_call(
        kern,
        grid=(t // FFN_TM,),
        in_specs=[
            pl.BlockSpec((FFN_TM, d), lambda i: (i, 0)),
            resident((1, d)),
            resident((d, D_FF)),
            resident((d, D_FF)),
            resident((FFN_CONV, D_FF)),
            resident((1, D_FF)),
            resident((D_FF, d)),
            resident((1, d)),
        ],
        out_specs=pl.BlockSpec((FFN_TM, d), lambda i: (i, 0)),
        out_shape=jax.ShapeDtypeStruct((t, d), F32),
        scratch_shapes=[pltpu.VMEM((V7X_SUBLANES, D_FF), F32)],
        name="ffn_final" if final_norm else "ffn",
        compiler_params=_params("arbitrary"),
    )(x2, gain, w_gate, w_up, conv_w, conv_b, w_down, final_gain)


def _block_diag(w):
    nh, hd, _ = w.shape
    eye = jnp.eye(nh, dtype=w.dtype)
    return (eye[:, None, :, None] * w[:, :, None, :]).reshape(nh * hd, nh * hd)


def kernel(x, mem, mix_norm_gain, w_in, lru_conv_w, lru_conv_b, lru_w_a, lru_b_a, lru_w_x, lru_b_x, lru_lambda,
           mem_norm_gain, w_mem_kv, w_branch, w_out, ffn_norm_gain, w_ffn_gate, w_ffn_up, ffn_conv_w, ffn_conv_b,
           w_ffn_down, final_norm_gain):
    bsz, s, d = x.shape
    depth = w_in.shape[0]
    x2 = x.reshape(bsz * s, d)
    slopes = jnp.exp2(-8.0 * jnp.arange(1, MOBA_HEADS + 1, dtype=F32) / MOBA_HEADS)
    row = lambda v: v.reshape(1, -1)
    for l in range(depth):
        w_proj = w_in[l, :, :PROJ_COLS].astype(BF16)
        w_glog = w_in[l, :, PROJ_COLS:].astype(BF16)
        w_gates = jnp.concatenate([_block_diag(lru_w_a[l]), _block_diag(lru_w_x[l])], axis=1).astype(BF16)
        b_gates = jnp.concatenate([lru_b_a[l].reshape(-1), lru_b_x[l].reshape(-1)]).reshape(1, -1)

        mk, mvt = _memkv(mem, row(mem_norm_gain[l]), w_mem_kv[l].astype(BF16))
        xg, qkv = _inproj(x2, row(mix_norm_gain[l]), w_proj)
        ya = _lru(xg, lru_conv_w[l], row(lru_conv_b[l]), w_gates, b_gates, row(lru_lambda[l]), bsz, s)
        yb = _moba(slopes, qkv, bsz, s)
        yc = _xattn(qkv, mk, mvt, bsz, s)
        x2 = _merge(x2, row(mix_norm_gain[l]), ya, yb, yc, w_glog, w_branch[l].astype(BF16), w_out[l].astype(BF16))
        x2 = _ffn(x2, row(ffn_norm_gain[l]), w_ffn_gate[l].astype(BF16), w_ffn_up[l].astype(BF16),
                  ffn_conv_w[l], row(ffn_conv_b[l]), w_ffn_down[l].astype(BF16), row(final_norm_gain),
                  s, final_norm=(l == depth - 1))
    return x2.reshape(bsz, s, d)
```

```python
import functools

import jax
import jax.numpy as jnp
from jax import lax
from jax.experimental import pallas as pl
from jax.experimental.pallas import tpu as pltpu

F32 = jnp.float32
BF16 = jnp.bfloat16

D_MODEL = 1024
LRU_WIDTH = 512
LRU_HEADS = 8
LRU_HEAD_DIM = LRU_WIDTH // LRU_HEADS
LRU_CONV = 4
LRU_C = 8.0
MOBA_HEADS = 8
MOBA_HEAD_DIM = 64
MOBA_WIDTH = MOBA_HEADS * MOBA_HEAD_DIM
MOBA_BLOCK = 256
MOBA_TOPK = 3
XATTN_HEADS = 4
XATTN_HEAD_DIM = 128
XATTN_WIDTH = XATTN_HEADS * XATTN_HEAD_DIM
N_BRANCH = 3
D_FF = 3 * D_MODEL
FFN_CONV = 3
NORM_EPS = 1e-6

V7X_SUBLANES = 8
V7X_LANES = 128
VMEM_LIMIT_BYTES = 56 * 1024 * 1024

PROJ_COLS = 2 * LRU_WIDTH + 3 * MOBA_WIDTH + XATTN_WIDTH
NEG_INF = float("-inf")


def _params(*semantics):
    return pltpu.CompilerParams(dimension_semantics=semantics, vmem_limit_bytes=VMEM_LIMIT_BYTES)


def _rms(x, g):
    return x * lax.rsqrt(jnp.mean(x * x, axis=-1, keepdims=True) + NORM_EPS) * g


def _nt_dot(a, b):
    return lax.dot_general(a, b, (((1,), (1,)), ((), ())), preferred_element_type=F32)


XATTN_DEN_ROWS = 2 * V7X_SUBLANES
XATTN_VT_ROWS = XATTN_HEAD_DIM + XATTN_DEN_ROWS

def _memkv_kernel(mem_ref, g_ref, w_ref, k_ref, vt_ref):
    h = _rms(mem_ref[0], g_ref[...]).astype(BF16)
    k_ref[0] = jnp.dot(h, w_ref[:, :XATTN_WIDTH], preferred_element_type=F32).astype(BF16)
    v = jnp.dot(h, w_ref[:, XATTN_WIDTH:], preferred_element_type=F32)
    ones_row = lax.broadcasted_iota(jnp.int32, (XATTN_DEN_ROWS, v.shape[0]), 0) == 0
    for hd in range(XATTN_HEADS):
        lo = hd * XATTN_VT_ROWS
        vt_ref[0, lo:lo + XATTN_HEAD_DIM, :] = v[:, hd * XATTN_HEAD_DIM:(hd + 1) * XATTN_HEAD_DIM].T.astype(BF16)
        vt_ref[0, lo + XATTN_HEAD_DIM:lo + XATTN_VT_ROWS, :] = jnp.where(ones_row, 1.0, 0.0).astype(BF16)


def _memkv(mem, gain, w):
    bsz, m, d = mem.shape
    return pl.pallas_call(
        _memkv_kernel,
        grid=(bsz,),
        in_specs=[
            pl.BlockSpec((1, m, d), lambda b: (b, 0, 0)),
            pl.BlockSpec((1, d), lambda b: (0, 0)),
            pl.BlockSpec((d, 2 * XATTN_WIDTH), lambda b: (0, 0)),
        ],
        out_specs=[
            pl.BlockSpec((1, m, XATTN_WIDTH), lambda b: (b, 0, 0)),
            pl.BlockSpec((1, XATTN_HEADS * XATTN_VT_ROWS, m), lambda b: (b, 0, 0)),
        ],
        out_shape=[
            jax.ShapeDtypeStruct((bsz, m, XATTN_WIDTH), BF16),
            jax.ShapeDtypeStruct((bsz, XATTN_HEADS * XATTN_VT_ROWS, m), BF16),
        ],
        name="memkv",
        compiler_params=_params("arbitrary"),
    )(mem, gain, w)


INPROJ_TM = 1024
INPROJ_CHUNK = 512


def _inproj_kernel(x_ref, g_ref, w_ref, xg_ref, qkv_ref):
    h = _rms(x_ref[...], g_ref[...]).astype(BF16)
    n_f32 = 2 * LRU_WIDTH // INPROJ_CHUNK
    for c in range(PROJ_COLS // INPROJ_CHUNK):
        lo = c * INPROJ_CHUNK
        y = jnp.dot(h, w_ref[:, lo:lo + INPROJ_CHUNK], preferred_element_type=F32)
        if c < n_f32:
            xg_ref[:, lo:lo + INPROJ_CHUNK] = y
        else:
            if c == n_f32:
                y = y * (MOBA_HEAD_DIM ** -0.5)
            o = lo - 2 * LRU_WIDTH
            qkv_ref[:, o:o + INPROJ_CHUNK] = y.astype(BF16)


def _inproj(x2, gain, w):
    t, d = x2.shape
    n_bf = PROJ_COLS - 2 * LRU_WIDTH
    return pl.pallas_call(
        _inproj_kernel,
        grid=(t // INPROJ_TM,),
        in_specs=[
            pl.BlockSpec((INPROJ_TM, d), lambda i: (i, 0)),
            pl.BlockSpec((1, d), lambda i: (0, 0)),
            pl.BlockSpec((d, PROJ_COLS), lambda i: (0, 0)),
        ],
        out_specs=[
            pl.BlockSpec((INPROJ_TM, 2 * LRU_WIDTH), lambda i: (i, 0)),
            pl.BlockSpec((INPROJ_TM, n_bf), lambda i: (i, 0)),
        ],
        out_shape=[
            jax.ShapeDtypeStruct((t, 2 * LRU_WIDTH), F32),
            jax.ShapeDtypeStruct((t, n_bf), BF16),
        ],
        name="inproj",
        compiler_params=_params("arbitrary"),
    )(x2, gain, w)


LRU_TM = 512
LRU_UNROLL = 8


def _lru_kernel(xg_ref, cw_ref, cb_ref, wg_ref, bg_ref, lam_ref, y_ref,
                xs_ref, a_ref, u_ref, h_ref, carry_ref):
    tm = LRU_TM
    w = LRU_WIDTH
    first = pl.program_id(1) == 0

    @pl.when(first)
    def _():
        xs_ref[0:V7X_SUBLANES, :] = jnp.zeros((V7X_SUBLANES, w), F32)
        carry_ref[...] = jnp.zeros((V7X_SUBLANES, w), F32)

    @pl.when(jnp.logical_not(first))
    def _():
        xs_ref[0:V7X_SUBLANES, :] = xs_ref[tm:tm + V7X_SUBLANES, :]

    xs_ref[V7X_SUBLANES:V7X_SUBLANES + tm, :] = xg_ref[:, :w]

    xc = cb_ref[...] + cw_ref[LRU_CONV - 1:LRU_CONV, :] * xs_ref[V7X_SUBLANES:V7X_SUBLANES + tm, :]
    for k in range(LRU_CONV - 1):
        off = V7X_SUBLANES - (LRU_CONV - 1) + k
        xc = xc + cw_ref[k:k + 1, :] * xs_ref[off:off + tm, :]

    gates = jnp.dot(xc.astype(BF16), wg_ref[...], preferred_element_type=F32) + bg_ref[...]
    r = jax.nn.sigmoid(gates[:, :w])
    i = jax.nn.sigmoid(gates[:, w:])
    z = -lam_ref[...]
    softplus = jnp.maximum(z, 0.0) + jnp.log1p(jnp.exp(-jnp.abs(z)))
    log_a = (-LRU_C) * r * softplus
    a = jnp.exp(log_a)
    th = jnp.tanh(log_a)
    mult = jnp.sqrt(-2.0 * th / (1.0 - th))
    a_ref[...] = a
    u_ref[...] = mult * i * xc

    sub = lax.broadcasted_iota(jnp.int32, (V7X_SUBLANES, w), 0)

    def body(g, carry):
        r0 = pl.multiple_of(g * V7X_SUBLANES, V7X_SUBLANES)
        av = a_ref[pl.ds(r0, V7X_SUBLANES), :]
        uv = u_ref[pl.ds(r0, V7X_SUBLANES), :]
        for d in (1, 2, 4):
            keep = sub >= d
            a_s = jnp.where(keep, pltpu.roll(av, d, 0), 1.0)
            u_s = jnp.where(keep, pltpu.roll(uv, d, 0), 0.0)
            uv = uv + av * u_s
            av = av * a_s
        hv = uv + av * carry
        h_ref[pl.ds(r0, V7X_SUBLANES), :] = hv
        return jnp.broadcast_to(hv[V7X_SUBLANES - 1:V7X_SUBLANES, :], (V7X_SUBLANES, w))

    carry_ref[...] = lax.fori_loop(0, tm // V7X_SUBLANES, body, carry_ref[...], unroll=LRU_UNROLL)
    y_ref[...] = (h_ref[...] * jax.nn.gelu(xg_ref[:, w:])).astype(BF16)


def _lru(xg, conv_w, conv_b, w_gates, b_gates, lam, bsz, s):
    t = xg.shape[0]
    w = LRU_WIDTH
    nt = s // LRU_TM
    return pl.pallas_call(
        _lru_kernel,
        grid=(bsz, nt),
        in_specs=[
            pl.BlockSpec((LRU_TM, 2 * w), lambda b, j: (b * nt + j, 0)),
            pl.BlockSpec((LRU_CONV, w), lambda b, j: (0, 0)),
            pl.BlockSpec((1, w), lambda b, j: (0, 0)),
            pl.BlockSpec((w, 2 * w), lambda b, j: (0, 0)),
            pl.BlockSpec((1, 2 * w), lambda b, j: (0, 0)),
            pl.BlockSpec((1, w), lambda b, j: (0, 0)),
        ],
        out_specs=pl.BlockSpec((LRU_TM, w), lambda b, j: (b * nt + j, 0)),
        out_shape=jax.ShapeDtypeStruct((t, w), BF16),
        scratch_shapes=[
            pltpu.VMEM((LRU_TM + 2 * V7X_SUBLANES, w), F32),
            pltpu.VMEM((LRU_TM, w), F32),
            pltpu.VMEM((LRU_TM, w), F32),
            pltpu.VMEM((LRU_TM, w), F32),
            pltpu.VMEM((V7X_SUBLANES, w), F32),
        ],
        name="lru",
        compiler_params=_params("arbitrary", "arbitrary"),
    )(xg, conv_w, conv_b, w_gates, b_gates, lam)


MOBA_PAIR = V7X_LANES // MOBA_HEAD_DIM


MOBA_AUX_POS, MOBA_AUX_BLK, MOBA_AUX_ONE = 0, 1, 2
MOBA_AUX_SEL = V7X_SUBLANES
MOBA_MASKED = -1e30
MOBA_DEN_ROWS = 2 * V7X_SUBLANES
MOBA_BOUND_SLACK = 1.02
MOBA_MIN_DEN = 1e-17


def _moba_kernel(slopes_ref, q_ref, k_ref, v_ref, o_ref, vt_ref, kw_ref, qw_ref, *, seq):
    nb = seq // MOBA_BLOCK
    blk = MOBA_BLOCK
    wide = MOBA_PAIR * blk
    assert MOBA_PAIR == 2 and nb <= V7X_SUBLANES and blk & (blk - 1) == 0
    hp = pl.program_id(1)
    lane = lax.broadcasted_iota(jnp.int32, (1, V7X_LANES), 1)

    @pl.when((pl.program_id(0) == 0) & (hp == 0))
    def _():
        rk = lax.broadcasted_iota(jnp.int32, (seq, V7X_LANES), 0)
        lk = lax.broadcasted_iota(jnp.int32, (seq, V7X_LANES), 1)
        pos = jnp.bitwise_and(rk, blk - 1)
        blk_idx = lax.shift_right_logical(rk, blk.bit_length() - 1)
        aux = jnp.where(lk == MOBA_AUX_POS, pos.astype(F32),
                        jnp.where(lk == MOBA_AUX_BLK, (rk - pos).astype(F32),
                                  jnp.where((lk == MOBA_AUX_ONE) | (lk - MOBA_AUX_SEL == blk_idx), 1.0, 0.0)))
        kw_ref[:, V7X_LANES:] = aux.astype(BF16)
        ones_row = lax.broadcasted_iota(jnp.int32, (MOBA_DEN_ROWS, seq), 0) == 0
        vt_ref[V7X_LANES:, :] = jnp.where(ones_row, 1.0, 0.0).astype(BF16)

    vt_ref[:V7X_LANES, :] = v_ref[...].astype(F32).T.astype(BF16)
    kb = k_ref[...]
    kw_ref[:, :V7X_LANES] = kb
    kf = kb.astype(F32)
    kmean = jnp.mean(kf.reshape(nb, blk, V7X_LANES), axis=1)
    km_hi = kmean.astype(BF16)
    km_lo = (kmean - km_hi.astype(F32)).astype(BF16)

    head_sum = (lax.shift_right_logical(lax.broadcasted_iota(jnp.int32, (V7X_LANES, V7X_LANES), 0),
                                        MOBA_HEAD_DIM.bit_length() - 1)
                == lax.broadcasted_iota(jnp.int32, (V7X_LANES, V7X_LANES), 1))
    kn2 = jnp.dot((kf * kf).astype(BF16), jnp.where(head_sum, 1.0, 0.0).astype(BF16),
                  preferred_element_type=F32)
    kn2_blk = jnp.max(kn2.reshape(nb, blk, V7X_LANES), axis=1)
    kn_upto, running = [], None
    for j in range(nb):
        row = kn2_blk[j:j + 1, :]
        running = row if running is None else jnp.maximum(running, row)
        kn_upto.append(jnp.sqrt(running))

    rowid = lax.broadcasted_iota(jnp.int32, (V7X_SUBLANES, wide), 0)
    colid = lax.broadcasted_iota(jnp.int32, (V7X_SUBLANES, wide), 1)
    first_head = colid < blk
    slope = jnp.where(first_head, slopes_ref[hp * MOBA_PAIR], slopes_ref[hp * MOBA_PAIR + 1])
    q_offset = jnp.bitwise_and(colid, blk - 1).astype(F32)
    key_row = lax.broadcasted_iota(jnp.int32, (blk, wide), 0)
    qry_col = jnp.bitwise_and(lax.broadcasted_iota(jnp.int32, (blk, wide), 1), blk - 1)
    causal = key_row <= qry_col
    ones_lhs = jnp.ones((V7X_SUBLANES, V7X_LANES), BF16)

    def query_operand(j):
        qj = q_ref[j * blk:(j + 1) * blk, :]
        zero = jnp.zeros_like(qj)
        q_heads = jnp.concatenate([jnp.where(lane < MOBA_HEAD_DIM, qj, zero),
                                   jnp.where(lane < MOBA_HEAD_DIM, zero, qj)], axis=0)
        q_norm = jnp.sqrt(_nt_dot(ones_lhs, jnp.square(q_heads.astype(F32)).astype(BF16)))
        k_norm = jnp.where(first_head,
                           jnp.sum(jnp.where(lane == 0, kn_upto[j], 0.0), axis=1, keepdims=True),
                           jnp.sum(jnp.where(lane == 1, kn_upto[j], 0.0), axis=1, keepdims=True))
        bound = MOBA_BOUND_SLACK * q_norm * k_norm + slope * q_offset
        q_rows = jnp.where(rowid <= MOBA_AUX_BLK, slope,
                           jnp.where(rowid == MOBA_AUX_ONE, -slope * (j * blk) - bound, 0.0))
        sel_rows = jnp.zeros((V7X_SUBLANES, wide), F32)
        if j > MOBA_TOPK:
            gate = _nt_dot(km_hi, q_heads) + _nt_dot(km_lo, q_heads)
            if nb < V7X_SUBLANES:
                gate = jnp.concatenate([gate, jnp.zeros((V7X_SUBLANES - nb, wide), F32)], axis=0)
            cnt = jnp.zeros((V7X_SUBLANES, wide), F32)
            for mth in range(j):
                gm = gate[mth:mth + 1, :]
                beats = (gm > gate) | ((gm == gate) & (rowid > mth))
                cnt = cnt + jnp.where(beats, 1.0, 0.0)
            sel_rows = jnp.where((rowid < j) & (cnt >= MOBA_TOPK), MOBA_MASKED, 0.0)
        q_aux = jnp.concatenate(
            [q_rows, sel_rows, jnp.zeros((V7X_LANES - 2 * V7X_SUBLANES, wide), F32)], axis=0).T.astype(BF16)
        qw_ref[j] = jnp.concatenate([q_heads, q_aux], axis=1)

    def shifted_scores(j):
        q_wide = qw_ref[j]
        st_own = _nt_dot(kw_ref[j * blk:(j + 1) * blk, :], q_wide)
        st_own = jnp.where(causal, st_own, NEG_INF)
        st_past = _nt_dot(kw_ref[0:j * blk, :], q_wide) if j > 0 else None
        return st_own, st_past

    def attend(j, st_own, st_past, exponent):
        acc = jnp.dot(vt_ref[:, j * blk:(j + 1) * blk], exponent(st_own), preferred_element_type=F32)
        if j > 0:
            acc = acc + jnp.dot(vt_ref[:, 0:j * blk], exponent(st_past), preferred_element_type=F32)
        den = acc[V7X_LANES:V7X_LANES + 1, :]
        ot = acc[:V7X_LANES, :] / den
        o_t = jnp.concatenate([ot[:MOBA_HEAD_DIM, :blk], ot[MOBA_HEAD_DIM:, blk:]], axis=0)
        o_ref[j * blk:(j + 1) * blk, :] = o_t.T.astype(BF16)
        return den

    den_min = None
    for j in range(nb):
        query_operand(j)
    for j in range(nb):
        den = attend(j, *shifted_scores(j), lambda st: jnp.exp(st).astype(BF16))
        den_min = den if den_min is None else jnp.minimum(den_min, den)

    @pl.when(jnp.logical_not(jnp.min(den_min) >= MOBA_MIN_DEN))
    def _():
        for j in range(nb):
            st_own, st_past = shifted_scores(j)
            m = jnp.max(st_own, axis=0, keepdims=True)
            if j > 0:
                m = jnp.maximum(m, jnp.max(st_past, axis=0, keepdims=True))
            attend(j, st_own, st_past, lambda st, m=m: jnp.exp(st - m).astype(BF16))


def _moba(slopes, qkv, bsz, s):
    t = qkv.shape[0]
    n_pairs = MOBA_HEADS // MOBA_PAIR
    return pl.pallas_call(
        functools.partial(_moba_kernel, seq=s),
        grid_spec=pltpu.PrefetchScalarGridSpec(
            num_scalar_prefetch=1,
            grid=(bsz, n_pairs),
            in_specs=[
                pl.BlockSpec((s, V7X_LANES), lambda b, p, sl: (b, p)),
                pl.BlockSpec((s, V7X_LANES), lambda b, p, sl: (b, n_pairs + p)),
                pl.BlockSpec((s, V7X_LANES), lambda b, p, sl: (b, 2 * n_pairs + p)),
            ],
            out_specs=pl.BlockSpec((s, V7X_LANES), lambda b, p, sl: (b, p)),
            scratch_shapes=[
                pltpu.VMEM((V7X_LANES + MOBA_DEN_ROWS, s), BF16),
                pltpu.VMEM((s, 2 * V7X_LANES), BF16),
                pltpu.VMEM((s // MOBA_BLOCK, MOBA_PAIR * MOBA_BLOCK, 2 * V7X_LANES), BF16),
            ],
        ),
        out_shape=jax.ShapeDtypeStruct((t, MOBA_WIDTH), BF16),
        name="moba",
        compiler_params=_params("arbitrary", "arbitrary"),
    )(slopes, qkv, qkv, qkv)


XATTN_TQ = 512


def _xattn_kernel(q_ref, k_ref, vt_ref, o_ref):
    scale = XATTN_HEAD_DIM ** -0.5
    heads = [slice(h * XATTN_HEAD_DIM, (h + 1) * XATTN_HEAD_DIM) for h in range(XATTN_HEADS)]
    scores = [_nt_dot(k_ref[0, :, hs], q_ref[:, hs]) * scale for hs in heads]
    for h, hs in enumerate(heads):
        st = scores[h]
        m = jnp.max(st, axis=0, keepdims=True)
        p = jnp.exp((st - m).astype(BF16))
        acc = jnp.dot(vt_ref[0, h * XATTN_VT_ROWS:(h + 1) * XATTN_VT_ROWS, :], p, preferred_element_type=F32)
        ot = acc[:XATTN_HEAD_DIM, :] / acc[XATTN_HEAD_DIM:XATTN_HEAD_DIM + 1, :]
        o_ref[:, hs] = ot.T.astype(BF16)


def _xattn(qkv, mk, mvt, bsz, s):
    t = qkv.shape[0]
    nt = s // XATTN_TQ
    m = mk.shape[1]
    qx_block = 3 * MOBA_WIDTH // XATTN_WIDTH
    return pl.pallas_call(
        _xattn_kernel,
        grid=(bsz, nt),
        in_specs=[
            pl.BlockSpec((XATTN_TQ, XATTN_WIDTH), lambda b, j: (b * nt + j, qx_block)),
            pl.BlockSpec((1, m, XATTN_WIDTH), lambda b, j: (b, 0, 0)),
            pl.BlockSpec((1, XATTN_HEADS * XATTN_VT_ROWS, m), lambda b, j: (b, 0, 0)),
        ],
        out_specs=pl.BlockSpec((XATTN_TQ, XATTN_WIDTH), lambda b, j: (b * nt + j, 0)),
        out_shape=jax.ShapeDtypeStruct((t, XATTN_WIDTH), BF16),
        name="xattn",
        compiler_params=_params("arbitrary", "arbitrary"),
    )(qkv, mk, mvt)


MERGE_TM = 1024


def _merge_kernel(x_ref, g_ref, ya_ref, yb_ref, yc_ref, wg_ref, wb_ref, wo_ref, o_ref):
    x = x_ref[...]
    h = _rms(x, g_ref[...]).astype(BF16)
    merged = None
    for n, y_ref in enumerate((ya_ref, yb_ref, yc_ref)):
        logits = jnp.dot(h, wg_ref[:, n * D_MODEL:(n + 1) * D_MODEL], preferred_element_type=F32)
        branch = jnp.dot(y_ref[...], wb_ref[n], preferred_element_type=F32)
        term = jax.nn.sigmoid(logits) * branch
        merged = term if merged is None else merged + term
    o_ref[...] = x + jnp.dot(merged.astype(BF16), wo_ref[...], preferred_element_type=F32)


def _merge(x2, gain, ya, yb, yc, w_gate, w_branch, w_out):
    t, d = x2.shape
    w = LRU_WIDTH
    return pl.pallas_call(
        _merge_kernel,
        grid=(t // MERGE_TM,),
        in_specs=[
            pl.BlockSpec((MERGE_TM, d), lambda i: (i, 0)),
            pl.BlockSpec((1, d), lambda i: (0, 0)),
            pl.BlockSpec((MERGE_TM, w), lambda i: (i, 0)),
            pl.BlockSpec((MERGE_TM, w), lambda i: (i, 0)),
            pl.BlockSpec((MERGE_TM, w), lambda i: (i, 0)),
            pl.BlockSpec((d, N_BRANCH * d), lambda i: (0, 0)),
            pl.BlockSpec((N_BRANCH, w, d), lambda i: (0, 0, 0)),
            pl.BlockSpec((d, d), lambda i: (0, 0)),
        ],
        out_specs=pl.BlockSpec((MERGE_TM, d), lambda i: (i, 0)),
        out_shape=jax.ShapeDtypeStruct((t, d), F32),
        name="merge",
        compiler_params=_params("arbitrary"),
    )(x2, gain, ya, yb, yc, w_gate, w_branch, w_out)


FFN_TM = 1024
FFN_GROUP = 1536
FFN_SUB = 512


def _ffn_kernel(x_ref, g_ref, wg_ref, wu_ref, cw_ref, cb_ref, wd_ref, fg_ref, o_ref, tail_ref,
                *, tiles_per_seq, final_norm):
    tm = FFN_TM
    i = pl.program_id(0)

    @pl.when(i == 0)
    def _():
        tail_ref[...] = jnp.zeros_like(tail_ref)

    x = x_ref[...]
    h = _rms(x, g_ref[...]).astype(BF16)
    sub = lax.broadcasted_iota(jnp.int32, (V7X_SUBLANES, FFN_SUB), 0)
    mid_seq = jnp.broadcast_to(i % tiles_per_seq, (V7X_SUBLANES, FFN_SUB)) != 0
    y = x
    for grp in range(D_FF // FFN_GROUP):
        acts = []
        for k in range(FFN_GROUP // FFN_SUB):
            lo = grp * FFN_GROUP + k * FFN_SUB
            g = jnp.dot(h, wg_ref[:, lo:lo + FFN_SUB], preferred_element_type=F32)
            u = jnp.dot(h, wu_ref[:, lo:lo + FFN_SUB], preferred_element_type=F32)
            prev = jnp.where(mid_seq, tail_ref[:, lo:lo + FFN_SUB], 0.0)
            tail_ref[:, lo:lo + FFN_SUB] = g[tm - V7X_SUBLANES:tm, :]
            conv = cb_ref[:, lo:lo + FFN_SUB] + cw_ref[FFN_CONV - 1:FFN_CONV, lo:lo + FFN_SUB] * g
            for shift in range(1, FFN_CONV):
                rolled = pltpu.roll(g, shift, 0)
                head = jnp.where(sub < shift, pltpu.roll(prev, shift, 0), rolled[0:V7X_SUBLANES, :])
                shifted = jnp.concatenate([head, rolled[V7X_SUBLANES:, :]], axis=0)
                tap = FFN_CONV - 1 - shift
                conv = conv + cw_ref[tap:tap + 1, lo:lo + FFN_SUB] * shifted
            acts.append((jax.nn.gelu(conv) * u).astype(BF16))
        act = jnp.concatenate(acts, axis=1)
        y = y + jnp.dot(act, wd_ref[grp * FFN_GROUP:(grp + 1) * FFN_GROUP, :], preferred_element_type=F32)
    if final_norm:
        y = _rms(y, fg_ref[...])
    o_ref[...] = y


def _ffn(x2, gain, w_gate, w_up, conv_w, conv_b, w_down, final_gain, s, final_norm):
    t, d = x2.shape
    kern = functools.partial(_ffn_kernel, tiles_per_seq=s // FFN_TM, final_norm=final_norm)
    resident = lambda shape: pl.BlockSpec(shape, lambda i: (0,) * len(shape), pipeline_mode=pl.Buffered(1))
    return pl.pallas_call(
        kern,
        grid=(t // FFN_TM,),
        in_specs=[
            pl.BlockSpec((FFN_TM, d), lambda i: (i, 0)),
            resident((1, d)),
            resident((d, D_FF)),
            resident((d, D_FF)),
            resident((FFN_CONV, D_FF)),
            resident((1, D_FF)),
            resident((D_FF, d)),
            resident((1, d)),
        ],
        out_specs=pl.BlockSpec((FFN_TM, d), lambda i: (i, 0)),
        out_shape=jax.ShapeDtypeStruct((t, d), F32),
        scratch_shapes=[pltpu.VMEM((V7X_SUBLANES, D_FF), F32)],
        name="ffn_final" if final_norm else "ffn",
        compiler_params=_params("arbitrary"),
    )(x2, gain, w_gate, w_up, conv_w, conv_b, w_down, final_gain)


def _block_diag(w):
    nh, hd, _ = w.shape
    eye = jnp.eye(nh, dtype=w.dtype)
    return (eye[:, None, :, None] * w[:, :, None, :]).reshape(nh * hd, nh * hd)


def kernel(x, mem, mix_norm_gain, w_in, lru_conv_w, lru_conv_b, lru_w_a, lru_b_a, lru_w_x, lru_b_x, lru_lambda,
           mem_norm_gain, w_mem_kv, w_branch, w_out, ffn_norm_gain, w_ffn_gate, w_ffn_up, ffn_conv_w, ffn_conv_b,
           w_ffn_down, final_norm_gain):
    bsz, s, d = x.shape
    depth = w_in.shape[0]
    x2 = x.reshape(bsz * s, d)
    slopes = jnp.exp2(-8.0 * jnp.arange(1, MOBA_HEADS + 1, dtype=F32) / MOBA_HEADS)
    row = lambda v: v.reshape(1, -1)
    for l in range(depth):
        w_proj = w_in[l, :, :PROJ_COLS].astype(BF16)
        w_glog = w_in[l, :, PROJ_COLS:].astype(BF16)
        w_gates = jnp.concatenate([_block_diag(lru_w_a[l]), _block_diag(lru_w_x[l])], axis=1).astype(BF16)
        b_gates = jnp.concatenate([lru_b_a[l].reshape(-1), lru_b_x[l].reshape(-1)]).reshape(1, -1)

        mk, mvt = _memkv(mem, row(mem_norm_gain[l]), w_mem_kv[l].astype(BF16))
        xg, qkv = _inproj(x2, row(mix_norm_gain[l]), w_proj)
        ya = _lru(xg, lru_conv_w[l], row(lru_conv_b[l]), w_gates, b_gates, row(lru_lambda[l]), bsz, s)
        yb = _moba(slopes, qkv, bsz, s)
        yc = _xattn(qkv, mk, mvt, bsz, s)
        x2 = _merge(x2, row(mix_norm_gain[l]), ya, yb, yc, w_glog, w_branch[l].astype(BF16), w_out[l].astype(BF16))
        x2 = _ffn(x2, row(ffn_norm_gain[l]), w_ffn_gate[l].astype(BF16), w_ffn_up[l].astype(BF16),
                  ffn_conv_w[l], row(ffn_conv_b[l]), w_ffn_down[l].astype(BF16), row(final_norm_gain),
                  s, final_norm=(l == depth - 1))
    return x2.reshape(bsz, s, d)
```

```python
import functools

import jax
import jax.numpy as jnp
from jax import lax
from jax.experimental import pallas as pl
from jax.experimental.pallas import tpu as pltpu

F32 = jnp.float32
BF16 = jnp.bfloat16

D_MODEL = 1024
LRU_WIDTH = 512
LRU_HEADS = 8
LRU_HEAD_DIM = LRU_WIDTH // LRU_HEADS
LRU_CONV = 4
LRU_C = 8.0
MOBA_HEADS = 8
MOBA_HEAD_DIM = 64
MOBA_WIDTH = MOBA_HEADS * MOBA_HEAD_DIM
MOBA_BLOCK = 256
MOBA_TOPK = 3
XATTN_HEADS = 4
XATTN_HEAD_DIM = 128
XATTN_WIDTH = XATTN_HEADS * XATTN_HEAD_DIM
N_BRANCH = 3
D_FF = 3 * D_MODEL
FFN_CONV = 3
NORM_EPS = 1e-6

V7X_SUBLANES = 8
V7X_LANES = 128
VMEM_LIMIT_BYTES = 56 * 1024 * 1024

PROJ_COLS = 2 * LRU_WIDTH + 3 * MOBA_WIDTH + XATTN_WIDTH
NEG_INF = float("-inf")


def _params(*semantics):
    return pltpu.CompilerParams(dimension_semantics=semantics, vmem_limit_bytes=VMEM_LIMIT_BYTES)


def _rms(x, g):
    return x * lax.rsqrt(jnp.mean(x * x, axis=-1, keepdims=True) + NORM_EPS) * g


def _nt_dot(a, b):
    return lax.dot_general(a, b, (((1,), (1,)), ((), ())), preferred_element_type=F32)


XATTN_DEN_ROWS = 2 * V7X_SUBLANES
XATTN_VT_ROWS = XATTN_HEAD_DIM + XATTN_DEN_ROWS

def _memkv_kernel(mem_ref, g_ref, w_ref, k_ref, vt_ref):
    h = _rms(mem_ref[0], g_ref[...]).astype(BF16)
    k_ref[0] = jnp.dot(h, w_ref[:, :XATTN_WIDTH], preferred_element_type=F32).astype(BF16)
    v = jnp.dot(h, w_ref[:, XATTN_WIDTH:], preferred_element_type=F32)
    ones_row = lax.broadcasted_iota(jnp.int32, (XATTN_DEN_ROWS, v.shape[0]), 0) == 0
    for hd in range(XATTN_HEADS):
        lo = hd * XATTN_VT_ROWS
        vt_ref[0, lo:lo + XATTN_HEAD_DIM, :] = v[:, hd * XATTN_HEAD_DIM:(hd + 1) * XATTN_HEAD_DIM].T.astype(BF16)
        vt_ref[0, lo + XATTN_HEAD_DIM:lo + XATTN_VT_ROWS, :] = jnp.where(ones_row, 1.0, 0.0).astype(BF16)


def _memkv(mem, gain, w):
    bsz, m, d = mem.shape
    return pl.pallas_call(
        _memkv_kernel,
        grid=(bsz,),
        in_specs=[
            pl.BlockSpec((1, m, d), lambda b: (b, 0, 0)),
            pl.BlockSpec((1, d), lambda b: (0, 0)),
            pl.BlockSpec((d, 2 * XATTN_WIDTH), lambda b: (0, 0)),
        ],
        out_specs=[
            pl.BlockSpec((1, m, XATTN_WIDTH), lambda b: (b, 0, 0)),
            pl.BlockSpec((1, XATTN_HEADS * XATTN_VT_ROWS, m), lambda b: (b, 0, 0)),
        ],
        out_shape=[
            jax.ShapeDtypeStruct((bsz, m, XATTN_WIDTH), BF16),
            jax.ShapeDtypeStruct((bsz, XATTN_HEADS * XATTN_VT_ROWS, m), BF16),
        ],
        name="memkv",
        compiler_params=_params("arbitrary"),
    )(mem, gain, w)


INPROJ_TM = 512
INPROJ_CHUNK = LRU_WIDTH


def _inproj_lru_kernel(x_ref, g_ref, w_ref, cw_ref, cb_ref, wg_ref, bg_ref, lam_ref, ya_ref, qkv_ref,
                       xs_ref, carry_ref, *, tiles_per_seq):
    tm = INPROJ_TM
    w = LRU_WIDTH
    first = (pl.program_id(0) % tiles_per_seq) == 0

    @pl.when(first)
    def _():
        xs_ref[0:V7X_SUBLANES, :] = jnp.zeros((V7X_SUBLANES, w), F32)
        carry_ref[...] = jnp.zeros((V7X_SUBLANES, w), F32)

    @pl.when(jnp.logical_not(first))
    def _():
        xs_ref[0:V7X_SUBLANES, :] = xs_ref[tm:tm + V7X_SUBLANES, :]

    h = _rms(x_ref[...], g_ref[...]).astype(BF16)

    def proj(c):
        return jnp.dot(h, w_ref[:, c * INPROJ_CHUNK:(c + 1) * INPROJ_CHUNK], preferred_element_type=F32)

    def put(c, y):
        o = (c - 2) * INPROJ_CHUNK
        qkv_ref[:, o:o + INPROJ_CHUNK] = y.astype(BF16)

    xs_ref[V7X_SUBLANES:V7X_SUBLANES + tm, :] = proj(0)
    ga = proj(1)
    put(2, proj(2) * (MOBA_HEAD_DIM ** -0.5))
    put(3, proj(3))

    xc = cb_ref[...] + cw_ref[LRU_CONV - 1:LRU_CONV, :] * xs_ref[V7X_SUBLANES:V7X_SUBLANES + tm, :]
    for k in range(LRU_CONV - 1):
        off = V7X_SUBLANES - (LRU_CONV - 1) + k
        xc = xc + cw_ref[k:k + 1, :] * xs_ref[off:off + tm, :]

    gates = jnp.dot(xc.astype(BF16), wg_ref[...], preferred_element_type=F32) + bg_ref[...]
    put(4, proj(4))
    put(5, proj(5))

    r = jax.nn.sigmoid(gates[:, :w])
    i = jax.nn.sigmoid(gates[:, w:])
    z = -lam_ref[...]
    softplus = jnp.maximum(z, 0.0) + jnp.log1p(jnp.exp(-jnp.abs(z)))
    log_a = (-LRU_C) * r * softplus
    a = jnp.exp(log_a)
    th = jnp.tanh(log_a)
    mult = jnp.sqrt(-2.0 * th / (1.0 - th))
    u = mult * i * xc

    sub = lax.broadcasted_iota(jnp.int32, (V7X_SUBLANES, w), 0)
    carry = carry_ref[...]
    hs = []
    for g in range(tm // V7X_SUBLANES):
        av = a[g * V7X_SUBLANES:(g + 1) * V7X_SUBLANES, :]
        uv = u[g * V7X_SUBLANES:(g + 1) * V7X_SUBLANES, :]
        for d in (1, 2, 4):
            keep = sub >= d
            a_s = jnp.where(keep, pltpu.roll(av, d, 0), 1.0)
            u_s = jnp.where(keep, pltpu.roll(uv, d, 0), 0.0)
            uv = uv + av * u_s
            av = av * a_s
        hv = uv + av * carry
        hs.append(hv)
        carry = jnp.broadcast_to(hv[V7X_SUBLANES - 1:V7X_SUBLANES, :], (V7X_SUBLANES, w))
    carry_ref[...] = carry
    ya_ref[...] = (jnp.concatenate(hs, axis=0) * jax.nn.gelu(ga)).astype(BF16)


def _inproj_lru(x2, gain, w_proj, conv_w, conv_b, w_gates, b_gates, lam, s):
    t, d = x2.shape
    w = LRU_WIDTH
    n_bf = PROJ_COLS - 2 * w
    resident = lambda shape: pl.BlockSpec(shape, lambda i: (0,) * len(shape), pipeline_mode=pl.Buffered(1))
    return pl.pallas_call(
        functools.partial(_inproj_lru_kernel, tiles_per_seq=s // INPROJ_TM),
        grid=(t // INPROJ_TM,),
        in_specs=[
            pl.BlockSpec((INPROJ_TM, d), lambda i: (i, 0)),
            resident((1, d)),
            resident((d, PROJ_COLS)),
            resident((LRU_CONV, w)),
            resident((1, w)),
            resident((w, 2 * w)),
            resident((1, 2 * w)),
            resident((1, w)),
        ],
        out_specs=[
            pl.BlockSpec((INPROJ_TM, w), lambda i: (i, 0)),
            pl.BlockSpec((INPROJ_TM, n_bf), lambda i: (i, 0)),
        ],
        out_shape=[
            jax.ShapeDtypeStruct((t, w), BF16),
            jax.ShapeDtypeStruct((t, n_bf), BF16),
        ],
        scratch_shapes=[
            pltpu.VMEM((INPROJ_TM + 2 * V7X_SUBLANES, w), F32),
            pltpu.VMEM((V7X_SUBLANES, w), F32),
        ],
        name="inproj_lru",
        compiler_params=_params("arbitrary"),
    )(x2, gain, w_proj, conv_w, conv_b, w_gates, b_gates, lam)


MOBA_PAIR = V7X_LANES // MOBA_HEAD_DIM


MOBA_AUX_POS, MOBA_AUX_BLK, MOBA_AUX_ONE = 0, 1, 2
MOBA_AUX_SEL = V7X_SUBLANES
MOBA_MASKED = -1e30
MOBA_DEN_ROWS = 2 * V7X_SUBLANES
MOBA_BOUND_SLACK = 1.02
MOBA_MIN_DEN = 1e-17


def _moba_kernel(slopes_ref, q_ref, k_ref, v_ref, o_ref, vt_ref, kw_ref, qw_ref, *, seq):
    nb = seq // MOBA_BLOCK
    blk = MOBA_BLOCK
    wide = MOBA_PAIR * blk
    assert MOBA_PAIR == 2 and nb <= V7X_SUBLANES and blk & (blk - 1) == 0
    hp = pl.program_id(1)
    lane = lax.broadcasted_iota(jnp.int32, (1, V7X_LANES), 1)

    @pl.when((pl.program_id(0) == 0) & (hp == 0))
    def _():
        rk = lax.broadcasted_iota(jnp.int32, (seq, V7X_LANES), 0)
        lk = lax.broadcasted_iota(jnp.int32, (seq, V7X_LANES), 1)
        pos = jnp.bitwise_and(rk, blk - 1)
        blk_idx = lax.shift_right_logical(rk, blk.bit_length() - 1)
        aux = jnp.where(lk == MOBA_AUX_POS, pos.astype(F32),
                        jnp.where(lk == MOBA_AUX_BLK, (rk - pos).astype(F32),
                                  jnp.where((lk == MOBA_AUX_ONE) | (lk - MOBA_AUX_SEL == blk_idx), 1.0, 0.0)))
        kw_ref[:, V7X_LANES:] = aux.astype(BF16)
        ones_row = lax.broadcasted_iota(jnp.int32, (MOBA_DEN_ROWS, seq), 0) == 0
        vt_ref[V7X_LANES:, :] = jnp.where(ones_row, 1.0, 0.0).astype(BF16)

    vt_ref[:V7X_LANES, :] = v_ref[...].astype(F32).T.astype(BF16)
    kb = k_ref[...]
    kw_ref[:, :V7X_LANES] = kb
    kf = kb.astype(F32)
    kmean = jnp.mean(kf.reshape(nb, blk, V7X_LANES), axis=1)
    km_hi = kmean.astype(BF16)
    km_lo = (kmean - km_hi.astype(F32)).astype(BF16)

    head_rows = (lax.shift_right_logical(lax.broadcasted_iota(jnp.int32, (V7X_SUBLANES, V7X_LANES), 1),
                                         MOBA_HEAD_DIM.bit_length() - 1)
                 == lax.broadcasted_iota(jnp.int32, (V7X_SUBLANES, V7X_LANES), 0))
    kn2 = _nt_dot(jnp.where(head_rows, 1.0, 0.0).astype(BF16), (kf * kf).astype(BF16))
    kn_upto, running = [], None
    for j in range(nb):
        blk_max = jnp.max(kn2[:, j * blk:(j + 1) * blk], axis=1, keepdims=True)
        running = blk_max if running is None else jnp.maximum(running, blk_max)
        kn_upto.append(jnp.sqrt(running))

    rowid = lax.broadcasted_iota(jnp.int32, (V7X_SUBLANES, wide), 0)
    colid = lax.broadcasted_iota(jnp.int32, (V7X_SUBLANES, wide), 1)
    first_head = colid < blk
    slope = jnp.where(first_head, slopes_ref[hp * MOBA_PAIR], slopes_ref[hp * MOBA_PAIR + 1])
    q_offset = jnp.bitwise_and(colid, blk - 1).astype(F32)
    key_row = lax.broadcasted_iota(jnp.int32, (blk, wide), 0)
    qry_col = jnp.bitwise_and(lax.broadcasted_iota(jnp.int32, (blk, wide), 1), blk - 1)
    causal = key_row <= qry_col
    ones_lhs = jnp.ones((V7X_SUBLANES, V7X_LANES), BF16)

    def query_operand(j):
        qj = q_ref[j * blk:(j + 1) * blk, :]
        zero = jnp.zeros_like(qj)
        q_heads = jnp.concatenate([jnp.where(lane < MOBA_HEAD_DIM, qj, zero),
                                   jnp.where(lane < MOBA_HEAD_DIM, zero, qj)], axis=0)
        q_norm = jnp.sqrt(_nt_dot(ones_lhs, jnp.square(q_heads.astype(F32)).astype(BF16)))
        k_norm = jnp.where(first_head, kn_upto[j][0:1, :], kn_upto[j][1:2, :])
        bound = MOBA_BOUND_SLACK * q_norm * k_norm + slope * q_offset
        q_rows = jnp.where(rowid <= MOBA_AUX_BLK, slope,
                           jnp.where(rowid == MOBA_AUX_ONE, -slope * (j * blk) - bound, 0.0))
        sel_rows = jnp.zeros((V7X_SUBLANES, wide), F32)
        if j > MOBA_TOPK:
            gate = _nt_dot(km_hi, q_heads) + _nt_dot(km_lo, q_heads)
            if nb < V7X_SUBLANES:
                gate = jnp.concatenate([gate, jnp.zeros((V7X_SUBLANES - nb, wide), F32)], axis=0)
            cnt = jnp.zeros((V7X_SUBLANES, wide), F32)
            for mth in range(j):
                gm = gate[mth:mth + 1, :]
                beats = (gm > gate) | ((gm == gate) & (rowid > mth))
                cnt = cnt + jnp.where(beats, 1.0, 0.0)
            sel_rows = jnp.where((rowid < j) & (cnt >= MOBA_TOPK), MOBA_MASKED, 0.0)
        q_aux = jnp.concatenate(
            [q_rows, sel_rows, jnp.zeros((V7X_LANES - 2 * V7X_SUBLANES, wide), F32)], axis=0).T.astype(BF16)
        qw_ref[j] = jnp.concatenate([q_heads, q_aux], axis=1)

    def shifted_scores(j):
        q_wide = qw_ref[j]
        st_own = _nt_dot(kw_ref[j * blk:(j + 1) * blk, :], q_wide)
        st_own = jnp.where(causal, st_own, NEG_INF)
        st_past = _nt_dot(kw_ref[0:j * blk, :], q_wide) if j > 0 else None
        return st_own, st_past

    def attend(j, st_own, st_past, exponent):
        acc = jnp.dot(vt_ref[:, j * blk:(j + 1) * blk], exponent(st_own), preferred_element_type=F32)
        if j > 0:
            acc = acc + jnp.dot(vt_ref[:, 0:j * blk], exponent(st_past), preferred_element_type=F32)
        den = acc[V7X_LANES:V7X_LANES + 1, :]
        ot = acc[:V7X_LANES, :] / den
        o_t = jnp.concatenate([ot[:MOBA_HEAD_DIM, :blk], ot[MOBA_HEAD_DIM:, blk:]], axis=0)
        o_ref[j * blk:(j + 1) * blk, :] = o_t.T.astype(BF16)
        return den

    den_min = None
    for j in range(nb):
        query_operand(j)
    for j in range(nb):
        den = attend(j, *shifted_scores(j), lambda st: jnp.exp(st).astype(BF16))
        den_min = den if den_min is None else jnp.minimum(den_min, den)

    @pl.when(jnp.logical_not(jnp.min(den_min) >= MOBA_MIN_DEN))
    def _():
        for j in range(nb):
            st_own, st_past = shifted_scores(j)
            m = jnp.max(st_own, axis=0, keepdims=True)
            if j > 0:
                m = jnp.maximum(m, jnp.max(st_past, axis=0, keepdims=True))
            attend(j, st_own, st_past, lambda st, m=m: jnp.exp(st - m).astype(BF16))


def _moba(slopes, qkv, bsz, s):
    t = qkv.shape[0]
    n_pairs = MOBA_HEADS // MOBA_PAIR
    return pl.pallas_call(
        functools.partial(_moba_kernel, seq=s),
        grid_spec=pltpu.PrefetchScalarGridSpec(
            num_scalar_prefetch=1,
            grid=(bsz, n_pairs),
            in_specs=[
                pl.BlockSpec((s, V7X_LANES), lambda b, p, sl: (b, p)),
                pl.BlockSpec((s, V7X_LANES), lambda b, p, sl: (b, n_pairs + p)),
                pl.BlockSpec((s, V7X_LANES), lambda b, p, sl: (b, 2 * n_pairs + p)),
            ],
            out_specs=pl.BlockSpec((s, V7X_LANES), lambda b, p, sl: (b, p)),
            scratch_shapes=[
                pltpu.VMEM((V7X_LANES + MOBA_DEN_ROWS, s), BF16),
                pltpu.VMEM((s, 2 * V7X_LANES), BF16),
                pltpu.VMEM((s // MOBA_BLOCK, MOBA_PAIR * MOBA_BLOCK, 2 * V7X_LANES), BF16),
            ],
        ),
        out_shape=jax.ShapeDtypeStruct((t, MOBA_WIDTH), BF16),
        name="moba",
        compiler_params=_params("arbitrary", "arbitrary"),
    )(slopes, qkv, qkv, qkv)


XATTN_TQ = 512


def _xattn_kernel(q_ref, k_ref, vt_ref, o_ref):
    scale = XATTN_HEAD_DIM ** -0.5
    heads = [slice(h * XATTN_HEAD_DIM, (h + 1) * XATTN_HEAD_DIM) for h in range(XATTN_HEADS)]
    scores = [_nt_dot(k_ref[0, :, hs], q_ref[:, hs]) * scale for hs in heads]
    for h, hs in enumerate(heads):
        st = scores[h]
        m = jnp.max(st, axis=0, keepdims=True)
        p = jnp.exp((st - m).astype(BF16))
        acc = jnp.dot(vt_ref[0, h * XATTN_VT_ROWS:(h + 1) * XATTN_VT_ROWS, :], p, preferred_element_type=F32)
        ot = acc[:XATTN_HEAD_DIM, :] / acc[XATTN_HEAD_DIM:XATTN_HEAD_DIM + 1, :]
        o_ref[:, hs] = ot.T.astype(BF16)


def _xattn(qkv, mk, mvt, bsz, s):
    t = qkv.shape[0]
    nt = s // XATTN_TQ
    m = mk.shape[1]
    qx_block = 3 * MOBA_WIDTH // XATTN_WIDTH
    return pl.pallas_call(
        _xattn_kernel,
        grid=(bsz, nt),
        in_specs=[
            pl.BlockSpec((XATTN_TQ, XATTN_WIDTH), lambda b, j: (b * nt + j, qx_block)),
            pl.BlockSpec((1, m, XATTN_WIDTH), lambda b, j: (b, 0, 0)),
            pl.BlockSpec((1, XATTN_HEADS * XATTN_VT_ROWS, m), lambda b, j: (b, 0, 0)),
        ],
        out_specs=pl.BlockSpec((XATTN_TQ, XATTN_WIDTH), lambda b, j: (b * nt + j, 0)),
        out_shape=jax.ShapeDtypeStruct((t, XATTN_WIDTH), BF16),
        name="xattn",
        compiler_params=_params("arbitrary", "arbitrary"),
    )(qkv, mk, mvt)


MERGE_TM = 1024


def _merge_kernel(x_ref, g_ref, ya_ref, yb_ref, yc_ref, wg_ref, wb_ref, wo_ref, o_ref):
    x = x_ref[...]
    h = _rms(x, g_ref[...]).astype(BF16)
    merged = None
    for n, y_ref in enumerate((ya_ref, yb_ref, yc_ref)):
        logits = jnp.dot(h, wg_ref[:, n * D_MODEL:(n + 1) * D_MODEL], preferred_element_type=F32)
        branch = jnp.dot(y_ref[...], wb_ref[n], preferred_element_type=F32)
        term = jax.nn.sigmoid(logits) * branch
        merged = term if merged is None else merged + term
    o_ref[...] = x + jnp.dot(merged.astype(BF16), wo_ref[...], preferred_element_type=F32)


def _merge(x2, gain, ya, yb, yc, w_gate, w_branch, w_out):
    t, d = x2.shape
    w = LRU_WIDTH
    return pl.pallas_call(
        _merge_kernel,
        grid=(t // MERGE_TM,),
        in_specs=[
            pl.BlockSpec((MERGE_TM, d), lambda i: (i, 0)),
            pl.BlockSpec((1, d), lambda i: (0, 0)),
            pl.BlockSpec((MERGE_TM, w), lambda i: (i, 0)),
            pl.BlockSpec((MERGE_TM, w), lambda i: (i, 0)),
            pl.BlockSpec((MERGE_TM, w), lambda i: (i, 0)),
            pl.BlockSpec((d, N_BRANCH * d), lambda i: (0, 0)),
            pl.BlockSpec((N_BRANCH, w, d), lambda i: (0, 0, 0)),
            pl.BlockSpec((d, d), lambda i: (0, 0)),
        ],
        out_specs=pl.BlockSpec((MERGE_TM, d), lambda i: (i, 0)),
        out_shape=jax.ShapeDtypeStruct((t, d), F32),
        name="merge",
        compiler_params=_params("arbitrary"),
    )(x2, gain, ya, yb, yc, w_gate, w_branch, w_out)


FFN_TM = 1024
FFN_GROUP = 1536
FFN_SUB = 512


def _ffn_kernel(x_ref, g_ref, wg_ref, wu_ref, cw_ref, cb_ref, wd_ref, fg_ref, o_ref, tail_ref,
                *, tiles_per_seq, final_norm):
    tm = FFN_TM
    i = pl.program_id(0)

    @pl.when(i == 0)
    def _():
        tail_ref[...] = jnp.zeros_like(tail_ref)

    x = x_ref[...]
    h = _rms(x, g_ref[...]).astype(BF16)
    sub = lax.broadcasted_iota(jnp.int32, (V7X_SUBLANES, FFN_SUB), 0)
    mid_seq = jnp.broadcast_to(i % tiles_per_seq, (V7X_SUBLANES, FFN_SUB)) != 0
    y = x
    for grp in range(D_FF // FFN_GROUP):
        acts = []
        for k in range(FFN_GROUP // FFN_SUB):
            lo = grp * FFN_GROUP + k * FFN_SUB
            g = jnp.dot(h, wg_ref[:, lo:lo + FFN_SUB], preferred_element_type=F32)
            u = jnp.dot(h, wu_ref[:, lo:lo + FFN_SUB], preferred_element_type=F32)
            prev = jnp.where(mid_seq, tail_ref[:, lo:lo + FFN_SUB], 0.0)
            tail_ref[:, lo:lo + FFN_SUB] = g[tm - V7X_SUBLANES:tm, :]
            conv = cb_ref[:, lo:lo + FFN_SUB] + cw_ref[FFN_CONV - 1:FFN_CONV, lo:lo + FFN_SUB] * g
            for shift in range(1, FFN_CONV):
                rolled = pltpu.roll(g, shift, 0)
                head = jnp.where(sub < shift, pltpu.roll(prev, shift, 0), rolled[0:V7X_SUBLANES, :])
                shifted = jnp.concatenate([head, rolled[V7X_SUBLANES:, :]], axis=0)
                tap = FFN_CONV - 1 - shift
                conv = conv + cw_ref[tap:tap + 1, lo:lo + FFN_SUB] * shifted
            acts.append((jax.nn.gelu(conv) * u).astype(BF16))
        act = jnp.concatenate(acts, axis=1)
        y = y + jnp.dot(act, wd_ref[grp * FFN_GROUP:(grp + 1) * FFN_GROUP, :], preferred_element_type=F32)
    if final_norm:
        y = _rms(y, fg_ref[...])
    o_ref[...] = y


def _ffn(x2, gain, w_gate, w_up, conv_w, conv_b, w_down, final_gain, s, final_norm):
    t, d = x2.shape
    kern = functools.partial(_ffn_kernel, tiles_per_seq=s // FFN_TM, final_norm=final_norm)
    resident = lambda shape: pl.BlockSpec(shape, lambda i: (0,) * len(shape), pipeline_mode=pl.Buffered(1))
    return pl.pallas_call(
        kern,
        grid=(t // FFN_TM,),
        in_specs=[
            pl.BlockSpec((FFN_TM, d), lambda i: (i, 0)),
            resident((1, d)),
            resident((d, D_FF)),
            resident((d, D_FF)),
            resident((FFN_CONV, D_FF)),
            resident((1, D_FF)),
            resident((D_FF, d)),
            resident((1, d)),
        ],
        out_specs=pl.BlockSpec((FFN_TM, d), lambda i: (i, 0)),
        out_shape=jax.ShapeDtypeStruct((t, d), F32),
        scratch_shapes=[pltpu.VMEM((V7X_SUBLANES, D_FF), F32)],
        name="ffn_final" if final_norm else "ffn",
        compiler_params=_params("arbitrary"),
    )(x2, gain, w_gate, w_up, conv_w, conv_b, w_down, final_gain)


def _block_diag(w):
    nh, hd, _ = w.shape
    eye = jnp.eye(nh, dtype=w.dtype)
    return (eye[:, None, :, None] * w[:, :, None, :]).reshape(nh * hd, nh * hd)


def kernel(x, mem, mix_norm_gain, w_in, lru_conv_w, lru_conv_b, lru_w_a, lru_b_a, lru_w_x, lru_b_x, lru_lambda,
           mem_norm_gain, w_mem_kv, w_branch, w_out, ffn_norm_gain, w_ffn_gate, w_ffn_up, ffn_conv_w, ffn_conv_b,
           w_ffn_down, final_norm_gain):
    bsz, s, d = x.shape
    depth = w_in.shape[0]
    x2 = x.reshape(bsz * s, d)
    slopes = jnp.exp2(-8.0 * jnp.arange(1, MOBA_HEADS + 1, dtype=F32) / MOBA_HEADS)
    row = lambda v: v.reshape(1, -1)
    for l in range(depth):
        w_proj = w_in[l, :, :PROJ_COLS].astype(BF16)
        w_glog = w_in[l, :, PROJ_COLS:].astype(BF16)
        w_gates = jnp.concatenate([_block_diag(lru_w_a[l]), _block_diag(lru_w_x[l])], axis=1).astype(BF16)
        b_gates = jnp.concatenate([lru_b_a[l].reshape(-1), lru_b_x[l].reshape(-1)]).reshape(1, -1)

        mk, mvt = _memkv(mem, row(mem_norm_gain[l]), w_mem_kv[l].astype(BF16))
        ya, qkv = _inproj_lru(x2, row(mix_norm_gain[l]), w_proj, lru_conv_w[l], row(lru_conv_b[l]), w_gates, b_gates,
                              row(lru_lambda[l]), s)
        yb = _moba(slopes, qkv, bsz, s)
        yc = _xattn(qkv, mk, mvt, bsz, s)
        x2 = _merge(x2, row(mix_norm_gain[l]), ya, yb, yc, w_glog, w_branch[l].astype(BF16), w_out[l].astype(BF16))
        x2 = _ffn(x2, row(ffn_norm_gain[l]), w_ffn_gate[l].astype(BF16), w_ffn_up[l].astype(BF16),
                  ffn_conv_w[l], row(ffn_conv_b[l]), w_ffn_down[l].astype(BF16), row(final_norm_gain),
                  s, final_norm=(l == depth - 1))
    return x2.reshape(bsz, s, d)
```

```python
import functools

import jax
import jax.numpy as jnp
from jax import lax
from jax.experimental import pallas as pl
from jax.experimental.pallas import tpu as pltpu

F32 = jnp.float32
BF16 = jnp.bfloat16

D_MODEL = 1024
LRU_WIDTH = 512
LRU_HEADS = 8
LRU_HEAD_DIM = LRU_WIDTH // LRU_HEADS
LRU_CONV = 4
LRU_C = 8.0
MOBA_HEADS = 8
MOBA_HEAD_DIM = 64
MOBA_WIDTH = MOBA_HEADS * MOBA_HEAD_DIM
MOBA_BLOCK = 256
MOBA_TOPK = 3
XATTN_HEADS = 4
XATTN_HEAD_DIM = 128
XATTN_WIDTH = XATTN_HEADS * XATTN_HEAD_DIM
N_BRANCH = 3
D_FF = 3 * D_MODEL
FFN_CONV = 3
NORM_EPS = 1e-6

V7X_SUBLANES = 8
V7X_LANES = 128
VMEM_LIMIT_BYTES = 56 * 1024 * 1024

PROJ_COLS = 2 * LRU_WIDTH + 3 * MOBA_WIDTH + XATTN_WIDTH
NEG_INF = float("-inf")


def _params(*semantics):
    return pltpu.CompilerParams(dimension_semantics=semantics, vmem_limit_bytes=VMEM_LIMIT_BYTES)


def _rms(x, g):
    return x * lax.rsqrt(jnp.mean(x * x, axis=-1, keepdims=True) + NORM_EPS) * g


def _nt_dot(a, b):
    return lax.dot_general(a, b, (((1,), (1,)), ((), ())), preferred_element_type=F32)


XATTN_DEN_ROWS = 2 * V7X_SUBLANES
XATTN_VT_ROWS = XATTN_HEAD_DIM + XATTN_DEN_ROWS

def _memkv_kernel(mem_ref, g_ref, w_ref, k_ref, vt_ref):
    h = _rms(mem_ref[0], g_ref[...]).astype(BF16)
    k_ref[0] = jnp.dot(h, w_ref[:, :XATTN_WIDTH], preferred_element_type=F32).astype(BF16)
    v = jnp.dot(h, w_ref[:, XATTN_WIDTH:], preferred_element_type=F32)
    ones_row = lax.broadcasted_iota(jnp.int32, (XATTN_DEN_ROWS, v.shape[0]), 0) == 0
    for hd in range(XATTN_HEADS):
        lo = hd * XATTN_VT_ROWS
        vt_ref[0, lo:lo + XATTN_HEAD_DIM, :] = v[:, hd * XATTN_HEAD_DIM:(hd + 1) * XATTN_HEAD_DIM].T.astype(BF16)
        vt_ref[0, lo + XATTN_HEAD_DIM:lo + XATTN_VT_ROWS, :] = jnp.where(ones_row, 1.0, 0.0).astype(BF16)


def _memkv(mem, gain, w, layer):
    bsz, m, d = mem.shape
    return pl.pallas_call(
        _memkv_kernel,
        grid=(bsz,),
        in_specs=[
            pl.BlockSpec((1, m, d), lambda b: (b, 0, 0)),
            pl.BlockSpec((1, d), lambda b: (0, 0)),
            pl.BlockSpec((None, d, 2 * XATTN_WIDTH), lambda b: (layer, 0, 0)),
        ],
        out_specs=[
            pl.BlockSpec((1, m, XATTN_WIDTH), lambda b: (b, 0, 0)),
            pl.BlockSpec((1, XATTN_HEADS * XATTN_VT_ROWS, m), lambda b: (b, 0, 0)),
        ],
        out_shape=[
            jax.ShapeDtypeStruct((bsz, m, XATTN_WIDTH), BF16),
            jax.ShapeDtypeStruct((bsz, XATTN_HEADS * XATTN_VT_ROWS, m), BF16),
        ],
        name="memkv",
        compiler_params=_params("arbitrary"),
    )(mem, gain, w)


INPROJ_TM = 1024
INPROJ_CHUNK = LRU_WIDTH


def _inproj_lru_kernel(x_ref, g_ref, w_ref, cw_ref, cb_ref, wg_ref, bg_ref, lam_ref, ya_ref, qkv_ref,
                       xs_ref, carry_ref, *, tiles_per_seq):
    tm = INPROJ_TM
    w = LRU_WIDTH
    first = (pl.program_id(0) % tiles_per_seq) == 0

    @pl.when(first)
    def _():
        xs_ref[0:V7X_SUBLANES, :] = jnp.zeros((V7X_SUBLANES, w), F32)
        carry_ref[...] = jnp.zeros((V7X_SUBLANES, w), F32)

    @pl.when(jnp.logical_not(first))
    def _():
        xs_ref[0:V7X_SUBLANES, :] = xs_ref[tm:tm + V7X_SUBLANES, :]

    h = _rms(x_ref[...], g_ref[...]).astype(BF16)

    def proj(c):
        return jnp.dot(h, w_ref[:, c * INPROJ_CHUNK:(c + 1) * INPROJ_CHUNK], preferred_element_type=F32)

    def put(c, y):
        o = (c - 2) * INPROJ_CHUNK
        qkv_ref[:, o:o + INPROJ_CHUNK] = y.astype(BF16)

    xs_ref[V7X_SUBLANES:V7X_SUBLANES + tm, :] = proj(0)
    ga = proj(1)
    put(2, proj(2) * (MOBA_HEAD_DIM ** -0.5))
    put(3, proj(3))

    xc = cb_ref[...] + cw_ref[LRU_CONV - 1:LRU_CONV, :] * xs_ref[V7X_SUBLANES:V7X_SUBLANES + tm, :]
    for k in range(LRU_CONV - 1):
        off = V7X_SUBLANES - (LRU_CONV - 1) + k
        xc = xc + cw_ref[k:k + 1, :] * xs_ref[off:off + tm, :]

    gates = jnp.dot(xc.astype(BF16), wg_ref[...], preferred_element_type=F32) + bg_ref[...]
    put(4, proj(4))
    put(5, proj(5))

    r = jax.nn.sigmoid(gates[:, :w])
    i = jax.nn.sigmoid(gates[:, w:])
    z = -lam_ref[...]
    softplus = jnp.maximum(z, 0.0) + jnp.log1p(jnp.exp(-jnp.abs(z)))
    log_a = (-LRU_C) * r * softplus
    a = jnp.exp(log_a)
    th = jnp.tanh(log_a)
    mult = jnp.sqrt(-2.0 * th / (1.0 - th))
    u = mult * i * xc

    sub = lax.broadcasted_iota(jnp.int32, (V7X_SUBLANES, w), 0)
    carry = carry_ref[...]
    hs = []
    for g in range(tm // V7X_SUBLANES):
        av = a[g * V7X_SUBLANES:(g + 1) * V7X_SUBLANES, :]
        uv = u[g * V7X_SUBLANES:(g + 1) * V7X_SUBLANES, :]
        for d in (1, 2, 4):
            keep = sub >= d
            a_s = jnp.where(keep, pltpu.roll(av, d, 0), 1.0)
            u_s = jnp.where(keep, pltpu.roll(uv, d, 0), 0.0)
            uv = uv + av * u_s
            av = av * a_s
        hv = uv + av * carry
        hs.append(hv)
        carry = jnp.broadcast_to(hv[V7X_SUBLANES - 1:V7X_SUBLANES, :], (V7X_SUBLANES, w))
    carry_ref[...] = carry
    ya_ref[...] = (jnp.concatenate(hs, axis=0) * jax.nn.gelu(ga)).astype(BF16)


def _inproj_lru(x2, gain, w_in, layer, conv_w, conv_b, w_gates, b_gates, lam, s):
    t, d = x2.shape
    w = LRU_WIDTH
    n_bf = PROJ_COLS - 2 * w
    resident = lambda shape: pl.BlockSpec(shape, lambda i: (0,) * len(shape), pipeline_mode=pl.Buffered(1))
    return pl.pallas_call(
        functools.partial(_inproj_lru_kernel, tiles_per_seq=s // INPROJ_TM),
        grid=(t // INPROJ_TM,),
        in_specs=[
            pl.BlockSpec((INPROJ_TM, d), lambda i: (i, 0)),
            resident((1, d)),
            pl.BlockSpec((None, d, PROJ_COLS), lambda i: (layer, 0, 0), pipeline_mode=pl.Buffered(1)),
            resident((LRU_CONV, w)),
            resident((1, w)),
            resident((w, 2 * w)),
            resident((1, 2 * w)),
            resident((1, w)),
        ],
        out_specs=[
            pl.BlockSpec((INPROJ_TM, w), lambda i: (i, 0)),
            pl.BlockSpec((INPROJ_TM, n_bf), lambda i: (i, 0)),
        ],
        out_shape=[
            jax.ShapeDtypeStruct((t, w), BF16),
            jax.ShapeDtypeStruct((t, n_bf), BF16),
        ],
        scratch_shapes=[
            pltpu.VMEM((INPROJ_TM + 2 * V7X_SUBLANES, w), F32),
            pltpu.VMEM((V7X_SUBLANES, w), F32),
        ],
        name="inproj_lru",
        compiler_params=_params("arbitrary"),
    )(x2, gain, w_in, conv_w, conv_b, w_gates, b_gates, lam)


MOBA_PAIR = V7X_LANES // MOBA_HEAD_DIM


MOBA_AUX_POS, MOBA_AUX_BLK, MOBA_AUX_ONE = 0, 1, 2
MOBA_AUX_SEL = V7X_SUBLANES
MOBA_MASKED = -1e30
MOBA_DEN_ROWS = 2 * V7X_SUBLANES
MOBA_BOUND_SLACK = 1.02
MOBA_MIN_DEN = 1e-17


def _moba_kernel(slopes_ref, q_ref, k_ref, v_ref, o_ref, vt_ref, kw_ref, qw_ref, *, seq):
    nb = seq // MOBA_BLOCK
    blk = MOBA_BLOCK
    wide = MOBA_PAIR * blk
    assert MOBA_PAIR == 2 and nb <= V7X_SUBLANES and blk & (blk - 1) == 0
    hp = pl.program_id(1)
    lane = lax.broadcasted_iota(jnp.int32, (1, V7X_LANES), 1)

    @pl.when((pl.program_id(0) == 0) & (hp == 0))
    def _():
        rk = lax.broadcasted_iota(jnp.int32, (seq, V7X_LANES), 0)
        lk = lax.broadcasted_iota(jnp.int32, (seq, V7X_LANES), 1)
        pos = jnp.bitwise_and(rk, blk - 1)
        blk_idx = lax.shift_right_logical(rk, blk.bit_length() - 1)
        aux = jnp.where(lk == MOBA_AUX_POS, pos.astype(F32),
                        jnp.where(lk == MOBA_AUX_BLK, (rk - pos).astype(F32),
                                  jnp.where((lk == MOBA_AUX_ONE) | (lk - MOBA_AUX_SEL == blk_idx), 1.0, 0.0)))
        kw_ref[:, V7X_LANES:] = aux.astype(BF16)
        ones_row = lax.broadcasted_iota(jnp.int32, (MOBA_DEN_ROWS, seq), 0) == 0
        vt_ref[V7X_LANES:, :] = jnp.where(ones_row, 1.0, 0.0).astype(BF16)

    vt_ref[:V7X_LANES, :] = v_ref[...].astype(F32).T.astype(BF16)
    kb = k_ref[...]
    kw_ref[:, :V7X_LANES] = kb
    kf = kb.astype(F32)
    kmean = jnp.mean(kf.reshape(nb, blk, V7X_LANES), axis=1)
    km_hi = kmean.astype(BF16)
    km_lo = (kmean - km_hi.astype(F32)).astype(BF16)

    head_sum = (lax.shift_right_logical(lax.broadcasted_iota(jnp.int32, (V7X_LANES, V7X_LANES), 0),
                                        MOBA_HEAD_DIM.bit_length() - 1)
                == lax.broadcasted_iota(jnp.int32, (V7X_LANES, V7X_LANES), 1))
    kn2 = jnp.dot((kf * kf).astype(BF16), jnp.where(head_sum, 1.0, 0.0).astype(BF16),
                  preferred_element_type=F32)
    kn2_blk = jnp.max(kn2.reshape(nb, blk, V7X_LANES), axis=1)
    kn_upto, running = [], None
    for j in range(nb):
        row = kn2_blk[j:j + 1, :]
        running = row if running is None else jnp.maximum(running, row)
        kn_upto.append(jnp.sqrt(running))

    rowid = lax.broadcasted_iota(jnp.int32, (V7X_SUBLANES, wide), 0)
    colid = lax.broadcasted_iota(jnp.int32, (V7X_SUBLANES, wide), 1)
    first_head = colid < blk
    slope = jnp.where(first_head, slopes_ref[hp * MOBA_PAIR], slopes_ref[hp * MOBA_PAIR + 1])
    q_offset = jnp.bitwise_and(colid, blk - 1).astype(F32)
    key_row = lax.broadcasted_iota(jnp.int32, (blk, wide), 0)
    qry_col = jnp.bitwise_and(lax.broadcasted_iota(jnp.int32, (blk, wide), 1), blk - 1)
    causal = key_row <= qry_col
    ones_lhs = jnp.ones((V7X_SUBLANES, V7X_LANES), BF16)

    def query_operand(j):
        qj = q_ref[j * blk:(j + 1) * blk, :]
        zero = jnp.zeros_like(qj)
        q_heads = jnp.concatenate([jnp.where(lane < MOBA_HEAD_DIM, qj, zero),
                                   jnp.where(lane < MOBA_HEAD_DIM, zero, qj)], axis=0)
        q_norm = jnp.sqrt(_nt_dot(ones_lhs, jnp.square(q_heads.astype(F32)).astype(BF16)))
        k_norm = jnp.where(first_head,
                           jnp.sum(jnp.where(lane == 0, kn_upto[j], 0.0), axis=1, keepdims=True),
                           jnp.sum(jnp.where(lane == 1, kn_upto[j], 0.0), axis=1, keepdims=True))
        bound = MOBA_BOUND_SLACK * q_norm * k_norm + slope * q_offset
        q_rows = jnp.where(rowid <= MOBA_AUX_BLK, slope,
                           jnp.where(rowid == MOBA_AUX_ONE, -slope * (j * blk) - bound, 0.0))
        sel_rows = jnp.zeros((V7X_SUBLANES, wide), F32)
        if j > MOBA_TOPK:
            gate = _nt_dot(km_hi, q_heads) + _nt_dot(km_lo, q_heads)
            if nb < V7X_SUBLANES:
                gate = jnp.concatenate([gate, jnp.zeros((V7X_SUBLANES - nb, wide), F32)], axis=0)
            cnt = jnp.zeros((V7X_SUBLANES, wide), F32)
            for mth in range(j):
                gm = gate[mth:mth + 1, :]
                beats = (gm > gate) | ((gm == gate) & (rowid > mth))
                cnt = cnt + jnp.where(beats, 1.0, 0.0)
            sel_rows = jnp.where((rowid < j) & (cnt >= MOBA_TOPK), MOBA_MASKED, 0.0)
        q_aux = jnp.concatenate(
            [q_rows, sel_rows, jnp.zeros((V7X_LANES - 2 * V7X_SUBLANES, wide), F32)], axis=0).T.astype(BF16)
        qw_ref[j] = jnp.concatenate([q_heads, q_aux], axis=1)

    def shifted_scores(j):
        q_wide = qw_ref[j]
        st_own = _nt_dot(kw_ref[j * blk:(j + 1) * blk, :], q_wide)
        st_own = jnp.where(causal, st_own, NEG_INF)
        st_past = _nt_dot(kw_ref[0:j * blk, :], q_wide) if j > 0 else None
        return st_own, st_past

    def attend(j, st_own, st_past, exponent):
        acc = jnp.dot(vt_ref[:, j * blk:(j + 1) * blk], exponent(st_own), preferred_element_type=F32)
        if j > 0:
            acc = acc + jnp.dot(vt_ref[:, 0:j * blk], exponent(st_past), preferred_element_type=F32)
        den = acc[V7X_LANES:V7X_LANES + 1, :]
        ot = acc[:V7X_LANES, :] / den
        o_t = jnp.concatenate([ot[:MOBA_HEAD_DIM, :blk], ot[MOBA_HEAD_DIM:, blk:]], axis=0)
        o_ref[j * blk:(j + 1) * blk, :] = o_t.T.astype(BF16)
        return den

    den_min = None
    for j in range(nb):
        query_operand(j)
    for j in range(nb):
        den = attend(j, *shifted_scores(j), lambda st: jnp.exp(st).astype(BF16))
        den_min = den if den_min is None else jnp.minimum(den_min, den)

    @pl.when(jnp.logical_not(jnp.min(den_min) >= MOBA_MIN_DEN))
    def _():
        for j in range(nb):
            st_own, st_past = shifted_scores(j)
            m = jnp.max(st_own, axis=0, keepdims=True)
            if j > 0:
                m = jnp.maximum(m, jnp.max(st_past, axis=0, keepdims=True))
            attend(j, st_own, st_past, lambda st, m=m: jnp.exp(st - m).astype(BF16))


def _moba(slopes, qkv, bsz, s):
    t = qkv.shape[0]
    n_pairs = MOBA_HEADS // MOBA_PAIR
    return pl.pallas_call(
        functools.partial(_moba_kernel, seq=s),
        grid_spec=pltpu.PrefetchScalarGridSpec(
            num_scalar_prefetch=1,
            grid=(bsz, n_pairs),
            in_specs=[
                pl.BlockSpec((s, V7X_LANES), lambda b, p, sl: (b, p)),
                pl.BlockSpec((s, V7X_LANES), lambda b, p, sl: (b, n_pairs + p)),
                pl.BlockSpec((s, V7X_LANES), lambda b, p, sl: (b, 2 * n_pairs + p)),
            ],
            out_specs=pl.BlockSpec((s, V7X_LANES), lambda b, p, sl: (b, p)),
            scratch_shapes=[
                pltpu.VMEM((V7X_LANES + MOBA_DEN_ROWS, s), BF16),
                pltpu.VMEM((s, 2 * V7X_LANES), BF16),
                pltpu.VMEM((s // MOBA_BLOCK, MOBA_PAIR * MOBA_BLOCK, 2 * V7X_LANES), BF16),
            ],
        ),
        out_shape=jax.ShapeDtypeStruct((t, MOBA_WIDTH), BF16),
        name="moba",
        compiler_params=_params("arbitrary", "arbitrary"),
    )(slopes, qkv, qkv, qkv)


XATTN_TQ = 512


def _xattn_kernel(q_ref, k_ref, vt_ref, o_ref):
    scale = XATTN_HEAD_DIM ** -0.5
    heads = [slice(h * XATTN_HEAD_DIM, (h + 1) * XATTN_HEAD_DIM) for h in range(XATTN_HEADS)]
    scores = [_nt_dot(k_ref[0, :, hs], q_ref[:, hs]) * scale for hs in heads]
    for h, hs in enumerate(heads):
        st = scores[h]
        m = jnp.max(st, axis=0, keepdims=True)
        p = jnp.exp((st - m).astype(BF16))
        acc = jnp.dot(vt_ref[0, h * XATTN_VT_ROWS:(h + 1) * XATTN_VT_ROWS, :], p, preferred_element_type=F32)
        ot = acc[:XATTN_HEAD_DIM, :] / acc[XATTN_HEAD_DIM:XATTN_HEAD_DIM + 1, :]
        o_ref[:, hs] = ot.T.astype(BF16)


def _xattn(qkv, mk, mvt, bsz, s):
    t = qkv.shape[0]
    nt = s // XATTN_TQ
    m = mk.shape[1]
    qx_block = 3 * MOBA_WIDTH // XATTN_WIDTH
    return pl.pallas_call(
        _xattn_kernel,
        grid=(bsz, nt),
        in_specs=[
            pl.BlockSpec((XATTN_TQ, XATTN_WIDTH), lambda b, j: (b * nt + j, qx_block)),
            pl.BlockSpec((1, m, XATTN_WIDTH), lambda b, j: (b, 0, 0)),
            pl.BlockSpec((1, XATTN_HEADS * XATTN_VT_ROWS, m), lambda b, j: (b, 0, 0)),
        ],
        out_specs=pl.BlockSpec((XATTN_TQ, XATTN_WIDTH), lambda b, j: (b * nt + j, 0)),
        out_shape=jax.ShapeDtypeStruct((t, XATTN_WIDTH), BF16),
        name="xattn",
        compiler_params=_params("arbitrary", "arbitrary"),
    )(qkv, mk, mvt)


MERGE_TM = 1024


def _merge_kernel(x_ref, g_ref, ya_ref, yb_ref, yc_ref, wg_ref, wb_ref, wo_ref, o_ref):
    x = x_ref[...]
    h = _rms(x, g_ref[...]).astype(BF16)
    merged = None
    for n, y_ref in enumerate((ya_ref, yb_ref, yc_ref)):
        logits = jnp.dot(h, wg_ref[:, n * D_MODEL:(n + 1) * D_MODEL], preferred_element_type=F32)
        branch = jnp.dot(y_ref[...], wb_ref[n], preferred_element_type=F32)
        term = jax.nn.sigmoid(logits) * branch
        merged = term if merged is None else merged + term
    o_ref[...] = x + jnp.dot(merged.astype(BF16), wo_ref[...], preferred_element_type=F32)


def _merge(x2, gain, ya, yb, yc, w_in, w_branch, w_out, layer):
    t, d = x2.shape
    w = LRU_WIDTH
    return pl.pallas_call(
        _merge_kernel,
        grid=(t // MERGE_TM,),
        in_specs=[
            pl.BlockSpec((MERGE_TM, d), lambda i: (i, 0)),
            pl.BlockSpec((1, d), lambda i: (0, 0)),
            pl.BlockSpec((MERGE_TM, w), lambda i: (i, 0)),
            pl.BlockSpec((MERGE_TM, w), lambda i: (i, 0)),
            pl.BlockSpec((MERGE_TM, w), lambda i: (i, 0)),
            pl.BlockSpec((None, d, N_BRANCH * d), lambda i: (layer, 0, PROJ_COLS // (N_BRANCH * d))),
            pl.BlockSpec((None, N_BRANCH, w, d), lambda i: (layer, 0, 0, 0)),
            pl.BlockSpec((None, d, d), lambda i: (layer, 0, 0)),
        ],
        out_specs=pl.BlockSpec((MERGE_TM, d), lambda i: (i, 0)),
        out_shape=jax.ShapeDtypeStruct((t, d), F32),
        name="merge",
        compiler_params=_params("arbitrary"),
    )(x2, gain, ya, yb, yc, w_in, w_branch, w_out)


FFN_TM = 1024
FFN_GROUP = 1536
FFN_SUB = 512


def _ffn_kernel(x_ref, g_ref, wg_ref, wu_ref, cw_ref, cb_ref, wd_ref, fg_ref, o_ref, tail_ref,
                *, tiles_per_seq, final_norm):
    tm = FFN_TM
    i = pl.program_id(0)

    @pl.when(i == 0)
    def _():
        tail_ref[...] = jnp.zeros_like(tail_ref)

    x = x_ref[...]
    h = _rms(x, g_ref[...]).astype(BF16)
    sub = lax.broadcasted_iota(jnp.int32, (V7X_SUBLANES, FFN_SUB), 0)
    mid_seq = jnp.broadcast_to(i % tiles_per_seq, (V7X_SUBLANES, FFN_SUB)) != 0
    y = x
    for grp in range(D_FF // FFN_GROUP):
        acts = []
        for k in range(FFN_GROUP // FFN_SUB):
            lo = grp * FFN_GROUP + k * FFN_SUB
            g = jnp.dot(h, wg_ref[:, lo:lo + FFN_SUB], preferred_element_type=F32)
            u = jnp.dot(h, wu_ref[:, lo:lo + FFN_SUB], preferred_element_type=F32)
            prev = jnp.where(mid_seq, tail_ref[:, lo:lo + FFN_SUB], 0.0)
            tail_ref[:, lo:lo + FFN_SUB] = g[tm - V7X_SUBLANES:tm, :]
            conv = cb_ref[:, lo:lo + FFN_SUB] + cw_ref[FFN_CONV - 1:FFN_CONV, lo:lo + FFN_SUB] * g
            for shift in range(1, FFN_CONV):
                rolled = pltpu.roll(g, shift, 0)
                head = jnp.where(sub < shift, pltpu.roll(prev, shift, 0), rolled[0:V7X_SUBLANES, :])
                shifted = jnp.concatenate([head, rolled[V7X_SUBLANES:, :]], axis=0)
                tap = FFN_CONV - 1 - shift
                conv = conv + cw_ref[tap:tap + 1, lo:lo + FFN_SUB] * shifted
            acts.append((jax.nn.gelu(conv) * u).astype(BF16))
        act = jnp.concatenate(acts, axis=1)
        y = y + jnp.dot(act, wd_ref[grp * FFN_GROUP:(grp + 1) * FFN_GROUP, :], preferred_element_type=F32)
    if final_norm:
        y = _rms(y, fg_ref[...])
    o_ref[...] = y


def _ffn(x2, gain, w_gate, w_up, conv_w, conv_b, w_down, final_gain, layer, s, final_norm):
    t, d = x2.shape
    kern = functools.partial(_ffn_kernel, tiles_per_seq=s // FFN_TM, final_norm=final_norm)
    resident = lambda shape: pl.BlockSpec(shape, lambda i: (0,) * len(shape), pipeline_mode=pl.Buffered(1))
    stacked = lambda shape: pl.BlockSpec((None,) + shape, lambda i: (layer,) + (0,) * len(shape),
                                         pipeline_mode=pl.Buffered(1))
    return pl.pallas_call(
        kern,
        grid=(t // FFN_TM,),
        in_specs=[
            pl.BlockSpec((FFN_TM, d), lambda i: (i, 0)),
            resident((1, d)),
            stacked((d, D_FF)),
            stacked((d, D_FF)),
            resident((FFN_CONV, D_FF)),
            resident((1, D_FF)),
            stacked((D_FF, d)),
            resident((1, d)),
        ],
        out_specs=pl.BlockSpec((FFN_TM, d), lambda i: (i, 0)),
        out_shape=jax.ShapeDtypeStruct((t, d), F32),
        scratch_shapes=[pltpu.VMEM((V7X_SUBLANES, D_FF), F32)],
        name="ffn_final" if final_norm else "ffn",
        compiler_params=_params("arbitrary"),
    )(x2, gain, w_gate, w_up, conv_w, conv_b, w_down, final_gain)


def _block_diag(w):
    nh, hd, _ = w.shape
    eye = jnp.eye(nh, dtype=w.dtype)
    return (eye[:, None, :, None] * w[:, :, None, :]).reshape(nh * hd, nh * hd)


def kernel(x, mem, mix_norm_gain, w_in, lru_conv_w, lru_conv_b, lru_w_a, lru_b_a, lru_w_x, lru_b_x, lru_lambda,
           mem_norm_gain, w_mem_kv, w_branch, w_out, ffn_norm_gain, w_ffn_gate, w_ffn_up, ffn_conv_w, ffn_conv_b,
           w_ffn_down, final_norm_gain):
    bsz, s, d = x.shape
    depth = w_in.shape[0]
    x2 = x.reshape(bsz * s, d)
    slopes = jnp.exp2(-8.0 * jnp.arange(1, MOBA_HEADS + 1, dtype=F32) / MOBA_HEADS)
    row = lambda v: v.reshape(1, -1)
    assert PROJ_COLS % (N_BRANCH * d) == 0
    w_in_b, w_mem_b, w_branch_b, w_out_b = (t.astype(BF16) for t in (w_in, w_mem_kv, w_branch, w_out))
    w_gate_b, w_up_b, w_down_b = (t.astype(BF16) for t in (w_ffn_gate, w_ffn_up, w_ffn_down))
    for l in range(depth):
        w_gates = jnp.concatenate([_block_diag(lru_w_a[l]), _block_diag(lru_w_x[l])], axis=1).astype(BF16)
        b_gates = jnp.concatenate([lru_b_a[l].reshape(-1), lru_b_x[l].reshape(-1)]).reshape(1, -1)

        mk, mvt = _memkv(mem, row(mem_norm_gain[l]), w_mem_b, l)
        ya, qkv = _inproj_lru(x2, row(mix_norm_gain[l]), w_in_b, l, lru_conv_w[l], row(lru_conv_b[l]), w_gates, b_gates,
                              row(lru_lambda[l]), s)
        yb = _moba(slopes, qkv, bsz, s)
        yc = _xattn(qkv, mk, mvt, bsz, s)
        x2 = _merge(x2, row(mix_norm_gain[l]), ya, yb, yc, w_in_b, w_branch_b, w_out_b, l)
        x2 = _ffn(x2, row(ffn_norm_gain[l]), w_gate_b, w_up_b, ffn_conv_w[l], row(ffn_conv_b[l]), w_down_b,
                  row(final_norm_gain), l, s, final_norm=(l == depth - 1))
    return x2.reshape(bsz, s, d)
```

```python
import functools

import jax
import jax.numpy as jnp
from jax import lax
from jax.experimental import pallas as pl
from jax.experimental.pallas import tpu as pltpu

F32 = jnp.float32
BF16 = jnp.bfloat16

D_MODEL = 1024
LRU_WIDTH = 512
LRU_HEADS = 8
LRU_HEAD_DIM = LRU_WIDTH // LRU_HEADS
LRU_CONV = 4
LRU_C = 8.0
MOBA_HEADS = 8
MOBA_HEAD_DIM = 64
MOBA_WIDTH = MOBA_HEADS * MOBA_HEAD_DIM
MOBA_BLOCK = 256
MOBA_TOPK = 3
XATTN_HEADS = 4
XATTN_HEAD_DIM = 128
XATTN_WIDTH = XATTN_HEADS * XATTN_HEAD_DIM
N_BRANCH = 3
D_FF = 3 * D_MODEL
FFN_CONV = 3
NORM_EPS = 1e-6

V7X_SUBLANES = 8
V7X_LANES = 128
VMEM_LIMIT_BYTES = 56 * 1024 * 1024

PROJ_COLS = 2 * LRU_WIDTH + 3 * MOBA_WIDTH + XATTN_WIDTH
NEG_INF = float("-inf")


def _params(*semantics):
    return pltpu.CompilerParams(dimension_semantics=semantics, vmem_limit_bytes=VMEM_LIMIT_BYTES)


def _rms(x, g):
    return x * lax.rsqrt(jnp.mean(x * x, axis=-1, keepdims=True) + NORM_EPS) * g


def _nt_dot(a, b):
    return lax.dot_general(a, b, (((1,), (1,)), ((), ())), preferred_element_type=F32)


XATTN_DEN_ROWS = 2 * V7X_SUBLANES
XATTN_VT_ROWS = XATTN_HEAD_DIM + XATTN_DEN_ROWS

def _memkv_kernel(mem_ref, g_ref, w_ref, k_ref, vt_ref):
    h = _rms(mem_ref[0], g_ref[...]).astype(BF16)
    k_ref[0] = jnp.dot(h, w_ref[:, :XATTN_WIDTH], preferred_element_type=F32).astype(BF16)
    v = jnp.dot(h, w_ref[:, XATTN_WIDTH:], preferred_element_type=F32)
    ones_row = lax.broadcasted_iota(jnp.int32, (XATTN_DEN_ROWS, v.shape[0]), 0) == 0
    for hd in range(XATTN_HEADS):
        lo = hd * XATTN_VT_ROWS
        vt_ref[0, lo:lo + XATTN_HEAD_DIM, :] = v[:, hd * XATTN_HEAD_DIM:(hd + 1) * XATTN_HEAD_DIM].T.astype(BF16)
        vt_ref[0, lo + XATTN_HEAD_DIM:lo + XATTN_VT_ROWS, :] = jnp.where(ones_row, 1.0, 0.0).astype(BF16)


def _memkv(mem, gain, w, layer):
    bsz, m, d = mem.shape
    return pl.pallas_call(
        _memkv_kernel,
        grid=(bsz,),
        in_specs=[
            pl.BlockSpec((1, m, d), lambda b: (b, 0, 0)),
            pl.BlockSpec((1, d), lambda b: (0, 0)),
            pl.BlockSpec((None, d, 2 * XATTN_WIDTH), lambda b: (layer, 0, 0)),
        ],
        out_specs=[
            pl.BlockSpec((1, m, XATTN_WIDTH), lambda b: (b, 0, 0)),
            pl.BlockSpec((1, XATTN_HEADS * XATTN_VT_ROWS, m), lambda b: (b, 0, 0)),
        ],
        out_shape=[
            jax.ShapeDtypeStruct((bsz, m, XATTN_WIDTH), BF16),
            jax.ShapeDtypeStruct((bsz, XATTN_HEADS * XATTN_VT_ROWS, m), BF16),
        ],
        name="memkv",
        compiler_params=_params("arbitrary"),
    )(mem, gain, w)


INPROJ_TM = 1024
INPROJ_CHUNK = LRU_WIDTH


def _inproj_lru_kernel(x_ref, g_ref, w_ref, cw_ref, cb_ref, wg_ref, bg_ref, lam_ref, ya_ref, qkv_ref,
                       xs_ref, carry_ref, *, tiles_per_seq):
    tm = INPROJ_TM
    w = LRU_WIDTH
    first = (pl.program_id(0) % tiles_per_seq) == 0

    @pl.when(first)
    def _():
        xs_ref[0:V7X_SUBLANES, :] = jnp.zeros((V7X_SUBLANES, w), F32)
        carry_ref[...] = jnp.zeros((V7X_SUBLANES, w), F32)

    @pl.when(jnp.logical_not(first))
    def _():
        xs_ref[0:V7X_SUBLANES, :] = xs_ref[tm:tm + V7X_SUBLANES, :]

    h = _rms(x_ref[...], g_ref[...]).astype(BF16)

    def proj(c):
        return jnp.dot(h, w_ref[:, c * INPROJ_CHUNK:(c + 1) * INPROJ_CHUNK], preferred_element_type=F32)

    def put(c, y):
        o = (c - 2) * INPROJ_CHUNK
        qkv_ref[:, o:o + INPROJ_CHUNK] = y.astype(BF16)

    xs_ref[V7X_SUBLANES:V7X_SUBLANES + tm, :] = proj(0)
    ga = proj(1)
    put(2, proj(2) * (MOBA_HEAD_DIM ** -0.5))
    put(3, proj(3))

    xc = cb_ref[...] + cw_ref[LRU_CONV - 1:LRU_CONV, :] * xs_ref[V7X_SUBLANES:V7X_SUBLANES + tm, :]
    for k in range(LRU_CONV - 1):
        off = V7X_SUBLANES - (LRU_CONV - 1) + k
        xc = xc + cw_ref[k:k + 1, :] * xs_ref[off:off + tm, :]

    gates = jnp.dot(xc.astype(BF16), wg_ref[...], preferred_element_type=F32) + bg_ref[...]
    put(4, proj(4))
    put(5, proj(5))

    r = jax.nn.sigmoid(gates[:, :w])
    i = jax.nn.sigmoid(gates[:, w:])
    z = -lam_ref[...]
    softplus = jnp.maximum(z, 0.0) + jnp.log1p(jnp.exp(-jnp.abs(z)))
    log_a = (-LRU_C) * r * softplus
    a = jnp.exp(log_a)
    th = jnp.tanh(log_a)
    mult = jnp.sqrt(-2.0 * th / (1.0 - th))
    u = mult * i * xc

    sub = lax.broadcasted_iota(jnp.int32, (V7X_SUBLANES, w), 0)
    carry = carry_ref[...]
    hs = []
    for g in range(tm // V7X_SUBLANES):
        av = a[g * V7X_SUBLANES:(g + 1) * V7X_SUBLANES, :]
        uv = u[g * V7X_SUBLANES:(g + 1) * V7X_SUBLANES, :]
        for d in (1, 2, 4):
            keep = sub >= d
            a_s = jnp.where(keep, pltpu.roll(av, d, 0), 1.0)
            u_s = jnp.where(keep, pltpu.roll(uv, d, 0), 0.0)
            uv = uv + av * u_s
            av = av * a_s
        hv = uv + av * carry
        hs.append(hv)
        carry = jnp.broadcast_to(hv[V7X_SUBLANES - 1:V7X_SUBLANES, :], (V7X_SUBLANES, w))
    carry_ref[...] = carry
    ya_ref[...] = (jnp.concatenate(hs, axis=0) * jax.nn.gelu(ga)).astype(BF16)


def _inproj_lru(x2, gain, w_in, layer, conv_w, conv_b, w_gates, b_gates, lam, s):
    t, d = x2.shape
    w = LRU_WIDTH
    n_bf = PROJ_COLS - 2 * w
    resident = lambda shape: pl.BlockSpec(shape, lambda i: (0,) * len(shape), pipeline_mode=pl.Buffered(1))
    return pl.pallas_call(
        functools.partial(_inproj_lru_kernel, tiles_per_seq=s // INPROJ_TM),
        grid=(t // INPROJ_TM,),
        in_specs=[
            pl.BlockSpec((INPROJ_TM, d), lambda i: (i, 0)),
            resident((1, d)),
            pl.BlockSpec((None, d, PROJ_COLS), lambda i: (layer, 0, 0), pipeline_mode=pl.Buffered(1)),
            resident((LRU_CONV, w)),
            resident((1, w)),
            resident((w, 2 * w)),
            resident((1, 2 * w)),
            resident((1, w)),
        ],
        out_specs=[
            pl.BlockSpec((INPROJ_TM, w), lambda i: (i, 0)),
            pl.BlockSpec((INPROJ_TM, n_bf), lambda i: (i, 0)),
        ],
        out_shape=[
            jax.ShapeDtypeStruct((t, w), BF16),
            jax.ShapeDtypeStruct((t, n_bf), BF16),
        ],
        scratch_shapes=[
            pltpu.VMEM((INPROJ_TM + 2 * V7X_SUBLANES, w), F32),
            pltpu.VMEM((V7X_SUBLANES, w), F32),
        ],
        name="inproj_lru",
        compiler_params=_params("arbitrary"),
    )(x2, gain, w_in, conv_w, conv_b, w_gates, b_gates, lam)


MOBA_PAIR = V7X_LANES // MOBA_HEAD_DIM


MOBA_AUX_POS, MOBA_AUX_BLK, MOBA_AUX_ONE = 0, 1, 2
MOBA_AUX_SEL = V7X_SUBLANES
MOBA_MASKED = -1e30
MOBA_DEN_ROWS = 2 * V7X_SUBLANES
MOBA_BOUND_SLACK = 1.02
MOBA_MIN_DEN = 1e-17


def _moba_kernel(slopes_ref, q_ref, k_ref, v_ref, o_ref, vt_ref, kw_ref, qw_ref, *, seq):
    nb = seq // MOBA_BLOCK
    blk = MOBA_BLOCK
    wide = MOBA_PAIR * blk
    assert MOBA_PAIR == 2 and nb <= V7X_SUBLANES and blk & (blk - 1) == 0
    hp = pl.program_id(1)
    lane = lax.broadcasted_iota(jnp.int32, (1, V7X_LANES), 1)

    @pl.when((pl.program_id(0) == 0) & (hp == 0))
    def _():
        rk = lax.broadcasted_iota(jnp.int32, (seq, V7X_LANES), 0)
        lk = lax.broadcasted_iota(jnp.int32, (seq, V7X_LANES), 1)
        pos = jnp.bitwise_and(rk, blk - 1)
        blk_idx = lax.shift_right_logical(rk, blk.bit_length() - 1)
        aux = jnp.where(lk == MOBA_AUX_POS, pos.astype(F32),
                        jnp.where(lk == MOBA_AUX_BLK, (rk - pos).astype(F32),
                                  jnp.where((lk == MOBA_AUX_ONE) | (lk - MOBA_AUX_SEL == blk_idx), 1.0, 0.0)))
        kw_ref[:, V7X_LANES:] = aux.astype(BF16)
        ones_row = lax.broadcasted_iota(jnp.int32, (MOBA_DEN_ROWS, seq), 0) == 0
        vt_ref[V7X_LANES:, :] = jnp.where(ones_row, 1.0, 0.0).astype(BF16)

    vt_ref[:V7X_LANES, :] = v_ref[...].astype(F32).T.astype(BF16)
    kb = k_ref[...]
    kw_ref[:, :V7X_LANES] = kb
    kf = kb.astype(F32)
    kmean = jnp.mean(kf.reshape(nb, blk, V7X_LANES), axis=1)
    km_hi = kmean.astype(BF16)
    km_lo = (kmean - km_hi.astype(F32)).astype(BF16)

    head_sum = (lax.shift_right_logical(lax.broadcasted_iota(jnp.int32, (V7X_LANES, V7X_LANES), 0),
                                        MOBA_HEAD_DIM.bit_length() - 1)
                == lax.broadcasted_iota(jnp.int32, (V7X_LANES, V7X_LANES), 1))
    kn2 = jnp.dot((kf * kf).astype(BF16), jnp.where(head_sum, 1.0, 0.0).astype(BF16),
                  preferred_element_type=F32)
    kn2_blk = jnp.max(kn2.reshape(nb, blk, V7X_LANES), axis=1)
    kn_upto, running = [], None
    for j in range(nb):
        row = kn2_blk[j:j + 1, :]
        running = row if running is None else jnp.maximum(running, row)
        kn_upto.append(jnp.sqrt(running))

    rowid = lax.broadcasted_iota(jnp.int32, (V7X_SUBLANES, wide), 0)
    colid = lax.broadcasted_iota(jnp.int32, (V7X_SUBLANES, wide), 1)
    first_head = colid < blk
    slope = jnp.where(first_head, slopes_ref[hp * MOBA_PAIR], slopes_ref[hp * MOBA_PAIR + 1])
    q_offset = jnp.bitwise_and(colid, blk - 1).astype(F32)
    key_row = lax.broadcasted_iota(jnp.int32, (blk, wide), 0)
    qry_col = jnp.bitwise_and(lax.broadcasted_iota(jnp.int32, (blk, wide), 1), blk - 1)
    causal = key_row <= qry_col
    ones_lhs = jnp.ones((V7X_SUBLANES, V7X_LANES), BF16)

    def query_operand(j):
        qj = q_ref[j * blk:(j + 1) * blk, :]
        zero = jnp.zeros_like(qj)
        q_heads = jnp.concatenate([jnp.where(lane < MOBA_HEAD_DIM, qj, zero),
                                   jnp.where(lane < MOBA_HEAD_DIM, zero, qj)], axis=0)
        q_norm = jnp.sqrt(_nt_dot(ones_lhs, jnp.square(q_heads.astype(F32)).astype(BF16)))
        k_norm = jnp.where(first_head,
                           jnp.sum(jnp.where(lane == 0, kn_upto[j], 0.0), axis=1, keepdims=True),
                           jnp.sum(jnp.where(lane == 1, kn_upto[j], 0.0), axis=1, keepdims=True))
        bound = MOBA_BOUND_SLACK * q_norm * k_norm + slope * q_offset
        q_rows = jnp.where(rowid <= MOBA_AUX_BLK, slope,
                           jnp.where(rowid == MOBA_AUX_ONE, -slope * (j * blk) - bound, 0.0))
        sel_rows = jnp.zeros((V7X_SUBLANES, wide), F32)
        if j > MOBA_TOPK:
            gate = _nt_dot(km_hi, q_heads) + _nt_dot(km_lo, q_heads)
            if nb < V7X_SUBLANES:
                gate = jnp.concatenate([gate, jnp.zeros((V7X_SUBLANES - nb, wide), F32)], axis=0)
            cnt = jnp.zeros((V7X_SUBLANES, wide), F32)
            for mth in range(j):
                gm = gate[mth:mth + 1, :]
                beats = (gm > gate) | ((gm == gate) & (rowid > mth))
                cnt = cnt + jnp.where(beats, 1.0, 0.0)
            sel_rows = jnp.where((rowid < j) & (cnt >= MOBA_TOPK), MOBA_MASKED, 0.0)
        q_aux = jnp.concatenate(
            [q_rows, sel_rows, jnp.zeros((V7X_LANES - 2 * V7X_SUBLANES, wide), F32)], axis=0).T.astype(BF16)
        qw_ref[j] = jnp.concatenate([q_heads, q_aux], axis=1)

    def shifted_scores(j):
        q_wide = qw_ref[j]
        st_own = _nt_dot(kw_ref[j * blk:(j + 1) * blk, :], q_wide)
        st_own = jnp.where(causal, st_own, NEG_INF)
        st_past = _nt_dot(kw_ref[0:j * blk, :], q_wide) if j > 0 else None
        return st_own, st_past

    def attend(j, st_own, st_past, exponent):
        acc = jnp.dot(vt_ref[:, j * blk:(j + 1) * blk], exponent(st_own), preferred_element_type=F32)
        if j > 0:
            acc = acc + jnp.dot(vt_ref[:, 0:j * blk], exponent(st_past), preferred_element_type=F32)
        den = acc[V7X_LANES:V7X_LANES + 1, :]
        ot = acc[:V7X_LANES, :] / den
        o_t = jnp.concatenate([ot[:MOBA_HEAD_DIM, :blk], ot[MOBA_HEAD_DIM:, blk:]], axis=0)
        o_ref[j * blk:(j + 1) * blk, :] = o_t.T.astype(BF16)
        return den

    den_min = None
    for j in range(nb):
        query_operand(j)
    for j in range(nb):
        den = attend(j, *shifted_scores(j), lambda st: jnp.exp(st).astype(BF16))
        den_min = den if den_min is None else jnp.minimum(den_min, den)

    @pl.when(jnp.logical_not(jnp.min(den_min) >= MOBA_MIN_DEN))
    def _():
        for j in range(nb):
            st_own, st_past = shifted_scores(j)
            m = jnp.max(st_own, axis=0, keepdims=True)
            if j > 0:
                m = jnp.maximum(m, jnp.max(st_past, axis=0, keepdims=True))
            attend(j, st_own, st_past, lambda st, m=m: jnp.exp(st - m).astype(BF16))


def _moba(slopes, qkv, bsz, s):
    t = qkv.shape[0]
    n_pairs = MOBA_HEADS // MOBA_PAIR
    return pl.pallas_call(
        functools.partial(_moba_kernel, seq=s),
        grid_spec=pltpu.PrefetchScalarGridSpec(
            num_scalar_prefetch=1,
            grid=(bsz, n_pairs),
            in_specs=[
                pl.BlockSpec((s, V7X_LANES), lambda b, p, sl: (b, p)),
                pl.BlockSpec((s, V7X_LANES), lambda b, p, sl: (b, n_pairs + p)),
                pl.BlockSpec((s, V7X_LANES), lambda b, p, sl: (b, 2 * n_pairs + p)),
            ],
            out_specs=pl.BlockSpec((s, V7X_LANES), lambda b, p, sl: (b, p)),
            scratch_shapes=[
                pltpu.VMEM((V7X_LANES + MOBA_DEN_ROWS, s), BF16),
                pltpu.VMEM((s, 2 * V7X_LANES), BF16),
                pltpu.VMEM((s // MOBA_BLOCK, MOBA_PAIR * MOBA_BLOCK, 2 * V7X_LANES), BF16),
            ],
        ),
        out_shape=jax.ShapeDtypeStruct((t, MOBA_WIDTH), BF16),
        name="moba",
        compiler_params=_params("arbitrary", "arbitrary"),
    )(slopes, qkv, qkv, qkv)


XATTN_TQ = 1024


def _xattn_kernel(q_ref, k_ref, vt_ref, o_ref):
    scale = XATTN_HEAD_DIM ** -0.5
    heads = [slice(h * XATTN_HEAD_DIM, (h + 1) * XATTN_HEAD_DIM) for h in range(XATTN_HEADS)]
    scores = [_nt_dot(k_ref[0, :, hs], q_ref[:, hs]) * scale for hs in heads]
    for h, hs in enumerate(heads):
        st = scores[h]
        m = jnp.max(st, axis=0, keepdims=True)
        p = jnp.exp((st - m).astype(BF16))
        acc = jnp.dot(vt_ref[0, h * XATTN_VT_ROWS:(h + 1) * XATTN_VT_ROWS, :], p, preferred_element_type=F32)
        ot = acc[:XATTN_HEAD_DIM, :] / acc[XATTN_HEAD_DIM:XATTN_HEAD_DIM + 1, :]
        o_ref[:, hs] = ot.T.astype(BF16)


def _xattn(qkv, mk, mvt, bsz, s):
    t = qkv.shape[0]
    nt = s // XATTN_TQ
    m = mk.shape[1]
    qx_block = 3 * MOBA_WIDTH // XATTN_WIDTH
    return pl.pallas_call(
        _xattn_kernel,
        grid=(bsz, nt),
        in_specs=[
            pl.BlockSpec((XATTN_TQ, XATTN_WIDTH), lambda b, j: (b * nt + j, qx_block)),
            pl.BlockSpec((1, m, XATTN_WIDTH), lambda b, j: (b, 0, 0)),
            pl.BlockSpec((1, XATTN_HEADS * XATTN_VT_ROWS, m), lambda b, j: (b, 0, 0)),
        ],
        out_specs=pl.BlockSpec((XATTN_TQ, XATTN_WIDTH), lambda b, j: (b * nt + j, 0)),
        out_shape=jax.ShapeDtypeStruct((t, XATTN_WIDTH), BF16),
        name="xattn",
        compiler_params=_params("arbitrary", "arbitrary"),
    )(qkv, mk, mvt)


MERGE_TM = 1024


def _merge_kernel(x_ref, g_ref, ya_ref, yb_ref, yc_ref, wg_ref, wb_ref, wo_ref, o_ref):
    x = x_ref[...]
    h = _rms(x, g_ref[...]).astype(BF16)
    merged = None
    for n, y_ref in enumerate((ya_ref, yb_ref, yc_ref)):
        logits = jnp.dot(h, wg_ref[:, n * D_MODEL:(n + 1) * D_MODEL], preferred_element_type=F32)
        branch = jnp.dot(y_ref[...], wb_ref[n], preferred_element_type=F32)
        term = jax.nn.sigmoid(logits) * branch
        merged = term if merged is None else merged + term
    o_ref[...] = x + jnp.dot(merged.astype(BF16), wo_ref[...], preferred_element_type=F32)


def _merge(x2, gain, ya, yb, yc, w_in, w_branch, w_out, layer):
    t, d = x2.shape
    w = LRU_WIDTH
    return pl.pallas_call(
        _merge_kernel,
        grid=(t // MERGE_TM,),
        in_specs=[
            pl.BlockSpec((MERGE_TM, d), lambda i: (i, 0)),
            pl.BlockSpec((1, d), lambda i: (0, 0)),
            pl.BlockSpec((MERGE_TM, w), lambda i: (i, 0)),
            pl.BlockSpec((MERGE_TM, w), lambda i: (i, 0)),
            pl.BlockSpec((MERGE_TM, w), lambda i: (i, 0)),
            pl.BlockSpec((None, d, N_BRANCH * d), lambda i: (layer, 0, PROJ_COLS // (N_BRANCH * d))),
            pl.BlockSpec((None, N_BRANCH, w, d), lambda i: (layer, 0, 0, 0)),
            pl.BlockSpec((None, d, d), lambda i: (layer, 0, 0)),
        ],
        out_specs=pl.BlockSpec((MERGE_TM, d), lambda i: (i, 0)),
        out_shape=jax.ShapeDtypeStruct((t, d), F32),
        name="merge",
        compiler_params=_params("arbitrary"),
    )(x2, gain, ya, yb, yc, w_in, w_branch, w_out)


FFN_TM = 1024
FFN_GROUP = 1536
FFN_SUB = 768


def _ffn_kernel(x_ref, g_ref, wg_ref, wu_ref, cw_ref, cb_ref, wd_ref, fg_ref, o_ref, tail_ref,
                *, tiles_per_seq, final_norm):
    tm = FFN_TM
    i = pl.program_id(0)

    @pl.when(i == 0)
    def _():
        tail_ref[...] = jnp.zeros_like(tail_ref)

    x = x_ref[...]
    h = _rms(x, g_ref[...]).astype(BF16)
    sub = lax.broadcasted_iota(jnp.int32, (V7X_SUBLANES, FFN_SUB), 0)
    mid_seq = jnp.broadcast_to(i % tiles_per_seq, (V7X_SUBLANES, FFN_SUB)) != 0
    y = x
    for grp in range(D_FF // FFN_GROUP):
        acts = []
        for k in range(FFN_GROUP // FFN_SUB):
            lo = grp * FFN_GROUP + k * FFN_SUB
            g = jnp.dot(h, wg_ref[:, lo:lo + FFN_SUB], preferred_element_type=F32)
            u = jnp.dot(h, wu_ref[:, lo:lo + FFN_SUB], preferred_element_type=F32)
            prev = jnp.where(mid_seq, tail_ref[:, lo:lo + FFN_SUB], 0.0)
            tail_ref[:, lo:lo + FFN_SUB] = g[tm - V7X_SUBLANES:tm, :]
            conv = cb_ref[:, lo:lo + FFN_SUB] + cw_ref[FFN_CONV - 1:FFN_CONV, lo:lo + FFN_SUB] * g
            for shift in range(1, FFN_CONV):
                rolled = pltpu.roll(g, shift, 0)
                head = jnp.where(sub < shift, pltpu.roll(prev, shift, 0), rolled[0:V7X_SUBLANES, :])
                shifted = jnp.concatenate([head, rolled[V7X_SUBLANES:, :]], axis=0)
                tap = FFN_CONV - 1 - shift
                conv = conv + cw_ref[tap:tap + 1, lo:lo + FFN_SUB] * shifted
            acts.append((jax.nn.gelu(conv) * u).astype(BF16))
        act = jnp.concatenate(acts, axis=1)
        y = y + jnp.dot(act, wd_ref[grp * FFN_GROUP:(grp + 1) * FFN_GROUP, :], preferred_element_type=F32)
    if final_norm:
        y = _rms(y, fg_ref[...])
    o_ref[...] = y


def _ffn(x2, gain, w_gate, w_up, conv_w, conv_b, w_down, final_gain, layer, s, final_norm):
    t, d = x2.shape
    kern = functools.partial(_ffn_kernel, tiles_per_seq=s // FFN_TM, final_norm=final_norm)
    resident = lambda shape: pl.BlockSpec(shape, lambda i: (0,) * len(shape), pipeline_mode=pl.Buffered(1))
    stacked = lambda shape: pl.BlockSpec((None,) + shape, lambda i: (layer,) + (0,) * len(shape),
                                         pipeline_mode=pl.Buffered(1))
    return pl.pallas_call(
        kern,
        grid=(t // FFN_TM,),
        in_specs=[
            pl.BlockSpec((FFN_TM, d), lambda i: (i, 0)),
            resident((1, d)),
            stacked((d, D_FF)),
            stacked((d, D_FF)),
            resident((FFN_CONV, D_FF)),
            resident((1, D_FF)),
            stacked((D_FF, d)),
            resident((1, d)),
        ],
        out_specs=pl.BlockSpec((FFN_TM, d), lambda i: (i, 0)),
        out_shape=jax.ShapeDtypeStruct((t, d), F32),
        scratch_shapes=[pltpu.VMEM((V7X_SUBLANES, D_FF), F32)],
        name="ffn_final" if final_norm else "ffn",
        compiler_params=_params("arbitrary"),
    )(x2, gain, w_gate, w_up, conv_w, conv_b, w_down, final_gain)


def _block_diag(w):
    nh, hd, _ = w.shape
    eye = jnp.eye(nh, dtype=w.dtype)
    return (eye[:, None, :, None] * w[:, :, None, :]).reshape(nh * hd, nh * hd)


def kernel(x, mem, mix_norm_gain, w_in, lru_conv_w, lru_conv_b, lru_w_a, lru_b_a, lru_w_x, lru_b_x, lru_lambda,
           mem_norm_gain, w_mem_kv, w_branch, w_out, ffn_norm_gain, w_ffn_gate, w_ffn_up, ffn_conv_w, ffn_conv_b,
           w_ffn_down, final_norm_gain):
    bsz, s, d = x.shape
    depth = w_in.shape[0]
    x2 = x.reshape(bsz * s, d)
    slopes = jnp.exp2(-8.0 * jnp.arange(1, MOBA_HEADS + 1, dtype=F32) / MOBA_HEADS)
    row = lambda v: v.reshape(1, -1)
    assert PROJ_COLS % (N_BRANCH * d) == 0
    w_in_b, w_mem_b, w_branch_b, w_out_b = (t.astype(BF16) for t in (w_in, w_mem_kv, w_branch, w_out))
    w_gate_b, w_up_b, w_down_b = (t.astype(BF16) for t in (w_ffn_gate, w_ffn_up, w_ffn_down))
    for l in range(depth):
        w_gates = jnp.concatenate([_block_diag(lru_w_a[l]), _block_diag(lru_w_x[l])], axis=1).astype(BF16)
        b_gates = jnp.concatenate([lru_b_a[l].reshape(-1), lru_b_x[l].reshape(-1)]).reshape(1, -1)

        mk, mvt = _memkv(mem, row(mem_norm_gain[l]), w_mem_b, l)
        ya, qkv = _inproj_lru(x2, row(mix_norm_gain[l]), w_in_b, l, lru_conv_w[l], row(lru_conv_b[l]), w_gates, b_gates,
                              row(lru_lambda[l]), s)
        yb = _moba(slopes, qkv, bsz, s)
        yc = _xattn(qkv, mk, mvt, bsz, s)
        x2 = _merge(x2, row(mix_norm_gain[l]), ya, yb, yc, w_in_b, w_branch_b, w_out_b, l)
        x2 = _ffn(x2, row(ffn_norm_gain[l]), w_gate_b, w_up_b, ffn_conv_w[l], row(ffn_conv_b[l]), w_down_b,
                  row(final_norm_gain), l, s, final_norm=(l == depth - 1))
    return x2.reshape(bsz, s, d)
```

```python
import functools

import jax
import jax.numpy as jnp
from jax import lax
from jax.experimental import pallas as pl
from jax.experimental.pallas import tpu as pltpu

F32 = jnp.float32
BF16 = jnp.bfloat16

D_MODEL = 1024
LRU_WIDTH = 512
LRU_HEADS = 8
LRU_HEAD_DIM = LRU_WIDTH // LRU_HEADS
LRU_CONV = 4
LRU_C = 8.0
MOBA_HEADS = 8
MOBA_HEAD_DIM = 64
MOBA_WIDTH = MOBA_HEADS * MOBA_HEAD_DIM
MOBA_BLOCK = 256
MOBA_TOPK = 3
XATTN_HEADS = 4
XATTN_HEAD_DIM = 128
XATTN_WIDTH = XATTN_HEADS * XATTN_HEAD_DIM
N_BRANCH = 3
D_FF = 3 * D_MODEL
FFN_CONV = 3
NORM_EPS = 1e-6

V7X_SUBLANES = 8
V7X_LANES = 128
VMEM_LIMIT_BYTES = 56 * 1024 * 1024

PROJ_COLS = 2 * LRU_WIDTH + 3 * MOBA_WIDTH + XATTN_WIDTH
NEG_INF = float("-inf")


def _params(*semantics):
    return pltpu.CompilerParams(dimension_semantics=semantics, vmem_limit_bytes=VMEM_LIMIT_BYTES)


def _rms(x, g):
    return x * lax.rsqrt(jnp.mean(x * x, axis=-1, keepdims=True) + NORM_EPS) * g


def _nt_dot(a, b):
    return lax.dot_general(a, b, (((1,), (1,)), ((), ())), preferred_element_type=F32)


XATTN_DEN_ROWS = 2 * V7X_SUBLANES
XATTN_VT_ROWS = XATTN_HEAD_DIM + XATTN_DEN_ROWS

def _memkv_kernel(mem_ref, g_ref, w_ref, k_ref, vt_ref):
    h = _rms(mem_ref[0], g_ref[...]).astype(BF16)
    k_ref[0] = jnp.dot(h, w_ref[:, :XATTN_WIDTH], preferred_element_type=F32).astype(BF16)
    v = jnp.dot(h, w_ref[:, XATTN_WIDTH:], preferred_element_type=F32)
    ones_row = lax.broadcasted_iota(jnp.int32, (XATTN_DEN_ROWS, v.shape[0]), 0) == 0
    for hd in range(XATTN_HEADS):
        lo = hd * XATTN_VT_ROWS
        vt_ref[0, lo:lo + XATTN_HEAD_DIM, :] = v[:, hd * XATTN_HEAD_DIM:(hd + 1) * XATTN_HEAD_DIM].T.astype(BF16)
        vt_ref[0, lo + XATTN_HEAD_DIM:lo + XATTN_VT_ROWS, :] = jnp.where(ones_row, 1.0, 0.0).astype(BF16)


def _memkv(mem, gain, w, layer):
    bsz, m, d = mem.shape
    return pl.pallas_call(
        _memkv_kernel,
        grid=(bsz,),
        in_specs=[
            pl.BlockSpec((1, m, d), lambda b: (b, 0, 0)),
            pl.BlockSpec((1, d), lambda b: (0, 0)),
            pl.BlockSpec((None, d, 2 * XATTN_WIDTH), lambda b: (layer, 0, 0)),
        ],
        out_specs=[
            pl.BlockSpec((1, m, XATTN_WIDTH), lambda b: (b, 0, 0)),
            pl.BlockSpec((1, XATTN_HEADS * XATTN_VT_ROWS, m), lambda b: (b, 0, 0)),
        ],
        out_shape=[
            jax.ShapeDtypeStruct((bsz, m, XATTN_WIDTH), BF16),
            jax.ShapeDtypeStruct((bsz, XATTN_HEADS * XATTN_VT_ROWS, m), BF16),
        ],
        name="memkv",
        compiler_params=_params("arbitrary"),
    )(mem, gain, w)


INPROJ_TM = 1024
INPROJ_CHUNK = LRU_WIDTH


def _inproj_lru_kernel(x_ref, g_ref, w_ref, cw_ref, cb_ref, wg_ref, bg_ref, lam_ref, ya_ref, qkv_ref,
                       xs_ref, carry_ref, *, tiles_per_seq):
    tm = INPROJ_TM
    w = LRU_WIDTH
    first = (pl.program_id(0) % tiles_per_seq) == 0

    @pl.when(first)
    def _():
        xs_ref[0:V7X_SUBLANES, :] = jnp.zeros((V7X_SUBLANES, w), F32)
        carry_ref[...] = jnp.zeros((V7X_SUBLANES, w), F32)

    @pl.when(jnp.logical_not(first))
    def _():
        xs_ref[0:V7X_SUBLANES, :] = xs_ref[tm:tm + V7X_SUBLANES, :]

    h = _rms(x_ref[...], g_ref[...]).astype(BF16)

    def proj(c):
        return jnp.dot(h, w_ref[:, c * INPROJ_CHUNK:(c + 1) * INPROJ_CHUNK], preferred_element_type=F32)

    def put(c, y):
        o = (c - 2) * INPROJ_CHUNK
        qkv_ref[:, o:o + INPROJ_CHUNK] = y.astype(BF16)

    xs_ref[V7X_SUBLANES:V7X_SUBLANES + tm, :] = proj(0)
    ga = proj(1)
    put(2, proj(2) * (MOBA_HEAD_DIM ** -0.5))
    put(3, proj(3))

    xc = cb_ref[...] + cw_ref[LRU_CONV - 1:LRU_CONV, :] * xs_ref[V7X_SUBLANES:V7X_SUBLANES + tm, :]
    for k in range(LRU_CONV - 1):
        off = V7X_SUBLANES - (LRU_CONV - 1) + k
        xc = xc + cw_ref[k:k + 1, :] * xs_ref[off:off + tm, :]

    gates = jnp.dot(xc.astype(BF16), wg_ref[...], preferred_element_type=F32) + bg_ref[...]
    put(4, proj(4))
    put(5, proj(5))

    r = jax.nn.sigmoid(gates[:, :w])
    i = jax.nn.sigmoid(gates[:, w:])
    z = -lam_ref[...]
    softplus = jnp.maximum(z, 0.0) + jnp.log1p(jnp.exp(-jnp.abs(z)))
    log_a = (-LRU_C) * r * softplus
    a = jnp.exp(log_a)
    th = jnp.tanh(log_a)
    mult = jnp.sqrt(-2.0 * th / (1.0 - th))
    u = mult * i * xc

    sub = lax.broadcasted_iota(jnp.int32, (V7X_SUBLANES, w), 0)
    carry = carry_ref[...]
    hs = []
    for g in range(tm // V7X_SUBLANES):
        av = a[g * V7X_SUBLANES:(g + 1) * V7X_SUBLANES, :]
        uv = u[g * V7X_SUBLANES:(g + 1) * V7X_SUBLANES, :]
        for d in (1, 2, 4):
            keep = sub >= d
            a_s = jnp.where(keep, pltpu.roll(av, d, 0), 1.0)
            u_s = jnp.where(keep, pltpu.roll(uv, d, 0), 0.0)
            uv = uv + av * u_s
            av = av * a_s
        hv = uv + av * carry
        hs.append(hv)
        carry = jnp.broadcast_to(hv[V7X_SUBLANES - 1:V7X_SUBLANES, :], (V7X_SUBLANES, w))
    carry_ref[...] = carry
    ya_ref[...] = (jnp.concatenate(hs, axis=0) * jax.nn.gelu(ga)).astype(BF16)


def _inproj_lru(x2, gain, w_in, layer, conv_w, conv_b, w_gates, b_gates, lam, s):
    t, d = x2.shape
    w = LRU_WIDTH
    n_bf = PROJ_COLS - 2 * w
    resident = lambda shape: pl.BlockSpec(shape, lambda i: (0,) * len(shape), pipeline_mode=pl.Buffered(1))
    return pl.pallas_call(
        functools.partial(_inproj_lru_kernel, tiles_per_seq=s // INPROJ_TM),
        grid=(t // INPROJ_TM,),
        in_specs=[
            pl.BlockSpec((INPROJ_TM, d), lambda i: (i, 0)),
            resident((1, d)),
            pl.BlockSpec((None, d, PROJ_COLS), lambda i: (layer, 0, 0), pipeline_mode=pl.Buffered(1)),
            resident((LRU_CONV, w)),
            resident((1, w)),
            resident((w, 2 * w)),
            resident((1, 2 * w)),
            resident((1, w)),
        ],
        out_specs=[
            pl.BlockSpec((INPROJ_TM, w), lambda i: (i, 0)),
            pl.BlockSpec((INPROJ_TM, n_bf), lambda i: (i, 0)),
        ],
        out_shape=[
            jax.ShapeDtypeStruct((t, w), BF16),
            jax.ShapeDtypeStruct((t, n_bf), BF16),
        ],
        scratch_shapes=[
            pltpu.VMEM((INPROJ_TM + 2 * V7X_SUBLANES, w), F32),
            pltpu.VMEM((V7X_SUBLANES, w), F32),
        ],
        name="inproj_lru",
        compiler_params=_params("arbitrary"),
    )(x2, gain, w_in, conv_w, conv_b, w_gates, b_gates, lam)


MOBA_PAIR = V7X_LANES // MOBA_HEAD_DIM


MOBA_AUX_POS, MOBA_AUX_BLK, MOBA_AUX_ONE = 0, 1, 2
MOBA_AUX_SEL = V7X_SUBLANES
MOBA_MASKED = -1e30
MOBA_DEN_ROWS = 2 * V7X_SUBLANES
MOBA_BOUND_SLACK = 1.02
MOBA_MIN_DEN = 1e-17


def _moba_kernel(slopes_ref, q_ref, k_ref, v_ref, o_ref, vt_ref, kw_ref, qw_ref, *, seq):
    nb = seq // MOBA_BLOCK
    blk = MOBA_BLOCK
    wide = MOBA_PAIR * blk
    assert MOBA_PAIR == 2 and nb <= V7X_SUBLANES and blk & (blk - 1) == 0
    hp = pl.program_id(1)
    lane = lax.broadcasted_iota(jnp.int32, (1, V7X_LANES), 1)

    @pl.when((pl.program_id(0) == 0) & (hp == 0))
    def _():
        rk = lax.broadcasted_iota(jnp.int32, (seq, V7X_LANES), 0)
        lk = lax.broadcasted_iota(jnp.int32, (seq, V7X_LANES), 1)
        pos = jnp.bitwise_and(rk, blk - 1)
        blk_idx = lax.shift_right_logical(rk, blk.bit_length() - 1)
        aux = jnp.where(lk == MOBA_AUX_POS, pos.astype(F32),
                        jnp.where(lk == MOBA_AUX_BLK, (rk - pos).astype(F32),
                                  jnp.where((lk == MOBA_AUX_ONE) | (lk - MOBA_AUX_SEL == blk_idx), 1.0, 0.0)))
        kw_ref[:, V7X_LANES:] = aux.astype(BF16)
        ones_row = lax.broadcasted_iota(jnp.int32, (MOBA_DEN_ROWS, seq), 0) == 0
        vt_ref[V7X_LANES:, :] = jnp.where(ones_row, 1.0, 0.0).astype(BF16)

    vt_ref[:V7X_LANES, :] = v_ref[...].astype(F32).T.astype(BF16)
    kb = k_ref[...]
    kw_ref[:, :V7X_LANES] = kb
    kf = kb.astype(F32)
    kmean = jnp.mean(kf.reshape(nb, blk, V7X_LANES), axis=1)
    km_hi = kmean.astype(BF16)
    km_lo = (kmean - km_hi.astype(F32)).astype(BF16)

    head_sum = (lax.shift_right_logical(lax.broadcasted_iota(jnp.int32, (V7X_LANES, V7X_LANES), 0),
                                        MOBA_HEAD_DIM.bit_length() - 1)
                == lax.broadcasted_iota(jnp.int32, (V7X_LANES, V7X_LANES), 1))
    kn2 = jnp.dot((kf * kf).astype(BF16), jnp.where(head_sum, 1.0, 0.0).astype(BF16),
                  preferred_element_type=F32)
    kn2_blk = jnp.max(kn2.reshape(nb, blk, V7X_LANES), axis=1)
    kn_upto, running = [], None
    for j in range(nb):
        row = kn2_blk[j:j + 1, :]
        running = row if running is None else jnp.maximum(running, row)
        kn_upto.append(jnp.sqrt(running))

    rowid = lax.broadcasted_iota(jnp.int32, (V7X_SUBLANES, wide), 0)
    colid = lax.broadcasted_iota(jnp.int32, (V7X_SUBLANES, wide), 1)
    first_head = colid < blk
    slope = jnp.where(first_head, slopes_ref[hp * MOBA_PAIR], slopes_ref[hp * MOBA_PAIR + 1])
    q_offset = jnp.bitwise_and(colid, blk - 1).astype(F32)
    key_row = lax.broadcasted_iota(jnp.int32, (blk, wide), 0)
    qry_col = jnp.bitwise_and(lax.broadcasted_iota(jnp.int32, (blk, wide), 1), blk - 1)
    causal = key_row <= qry_col
    ones_lhs = jnp.ones((V7X_SUBLANES, V7X_LANES), BF16)

    def query_operand(j):
        qj = q_ref[j * blk:(j + 1) * blk, :]
        zero = jnp.zeros_like(qj)
        q_heads = jnp.concatenate([jnp.where(lane < MOBA_HEAD_DIM, qj, zero),
                                   jnp.where(lane < MOBA_HEAD_DIM, zero, qj)], axis=0)
        q_norm = jnp.sqrt(_nt_dot(ones_lhs, jnp.square(q_heads.astype(F32)).astype(BF16)))
        k_norm = jnp.where(first_head,
                           jnp.sum(jnp.where(lane == 0, kn_upto[j], 0.0), axis=1, keepdims=True),
                           jnp.sum(jnp.where(lane == 1, kn_upto[j], 0.0), axis=1, keepdims=True))
        bound = MOBA_BOUND_SLACK * q_norm * k_norm + slope * q_offset
        q_rows = jnp.where(rowid <= MOBA_AUX_BLK, slope,
                           jnp.where(rowid == MOBA_AUX_ONE, -slope * (j * blk) - bound, 0.0))
        sel_rows = jnp.zeros((V7X_SUBLANES, wide), F32)
        if j > MOBA_TOPK:
            gate = _nt_dot(km_hi, q_heads) + _nt_dot(km_lo, q_heads)
            if nb < V7X_SUBLANES:
                gate = jnp.concatenate([gate, jnp.zeros((V7X_SUBLANES - nb, wide), F32)], axis=0)
            cnt = jnp.zeros((V7X_SUBLANES, wide), F32)
            for mth in range(j):
                gm = gate[mth:mth + 1, :]
                beats = (gm > gate) | ((gm == gate) & (rowid > mth))
                cnt = cnt + jnp.where(beats, 1.0, 0.0)
            sel_rows = jnp.where((rowid < j) & (cnt >= MOBA_TOPK), MOBA_MASKED, 0.0)
        q_aux = jnp.concatenate(
            [q_rows, sel_rows, jnp.zeros((V7X_LANES - 2 * V7X_SUBLANES, wide), F32)], axis=0).T.astype(BF16)
        qw_ref[j] = jnp.concatenate([q_heads, q_aux], axis=1)

    def shifted_scores(j):
        q_wide = qw_ref[j]
        st_own = _nt_dot(kw_ref[j * blk:(j + 1) * blk, :], q_wide)
        st_own = jnp.where(causal, st_own, NEG_INF)
        st_past = _nt_dot(kw_ref[0:j * blk, :], q_wide) if j > 0 else None
        return st_own, st_past

    def attend(j, st_own, st_past, exponent):
        acc = jnp.dot(vt_ref[:, j * blk:(j + 1) * blk], exponent(st_own), preferred_element_type=F32)
        if j > 0:
            acc = acc + jnp.dot(vt_ref[:, 0:j * blk], exponent(st_past), preferred_element_type=F32)
        den = acc[V7X_LANES:V7X_LANES + 1, :]
        ot = acc[:V7X_LANES, :] / den
        o_t = jnp.concatenate([ot[:MOBA_HEAD_DIM, :blk], ot[MOBA_HEAD_DIM:, blk:]], axis=0)
        o_ref[j * blk:(j + 1) * blk, :] = o_t.T.astype(BF16)
        return den

    den_min = None
    for j in range(nb):
        query_operand(j)
    pending = shifted_scores(0)
    for j in range(nb):
        upcoming = shifted_scores(j + 1) if j + 1 < nb else None
        den = attend(j, *pending, lambda st: jnp.exp(st).astype(BF16))
        den_min = den if den_min is None else jnp.minimum(den_min, den)
        pending = upcoming

    @pl.when(jnp.logical_not(jnp.min(den_min) >= MOBA_MIN_DEN))
    def _():
        for j in range(nb):
            st_own, st_past = shifted_scores(j)
            m = jnp.max(st_own, axis=0, keepdims=True)
            if j > 0:
                m = jnp.maximum(m, jnp.max(st_past, axis=0, keepdims=True))
            attend(j, st_own, st_past, lambda st, m=m: jnp.exp(st - m).astype(BF16))


def _moba(slopes, qkv, bsz, s):
    t = qkv.shape[0]
    n_pairs = MOBA_HEADS // MOBA_PAIR
    return pl.pallas_call(
        functools.partial(_moba_kernel, seq=s),
        grid_spec=pltpu.PrefetchScalarGridSpec(
            num_scalar_prefetch=1,
            grid=(bsz, n_pairs),
            in_specs=[
                pl.BlockSpec((s, V7X_LANES), lambda b, p, sl: (b, p)),
                pl.BlockSpec((s, V7X_LANES), lambda b, p, sl: (b, n_pairs + p)),
                pl.BlockSpec((s, V7X_LANES), lambda b, p, sl: (b, 2 * n_pairs + p)),
            ],
            out_specs=pl.BlockSpec((s, V7X_LANES), lambda b, p, sl: (b, p)),
            scratch_shapes=[
                pltpu.VMEM((V7X_LANES + MOBA_DEN_ROWS, s), BF16),
                pltpu.VMEM((s, 2 * V7X_LANES), BF16),
                pltpu.VMEM((s // MOBA_BLOCK, MOBA_PAIR * MOBA_BLOCK, 2 * V7X_LANES), BF16),
            ],
        ),
        out_shape=jax.ShapeDtypeStruct((t, MOBA_WIDTH), BF16),
        name="moba",
        compiler_params=_params("arbitrary", "arbitrary"),
    )(slopes, qkv, qkv, qkv)


XATTN_TQ = 1024


def _xattn_kernel(q_ref, k_ref, vt_ref, o_ref):
    scale = XATTN_HEAD_DIM ** -0.5
    heads = [slice(h * XATTN_HEAD_DIM, (h + 1) * XATTN_HEAD_DIM) for h in range(XATTN_HEADS)]
    scores = [_nt_dot(k_ref[0, :, hs], q_ref[:, hs]) * scale for hs in heads]
    for h, hs in enumerate(heads):
        st = scores[h]
        m = jnp.max(st, axis=0, keepdims=True)
        p = jnp.exp((st - m).astype(BF16))
        acc = jnp.dot(vt_ref[0, h * XATTN_VT_ROWS:(h + 1) * XATTN_VT_ROWS, :], p, preferred_element_type=F32)
        ot = acc[:XATTN_HEAD_DIM, :] / acc[XATTN_HEAD_DIM:XATTN_HEAD_DIM + 1, :]
        o_ref[:, hs] = ot.T.astype(BF16)


def _xattn(qkv, mk, mvt, bsz, s):
    t = qkv.shape[0]
    nt = s // XATTN_TQ
    m = mk.shape[1]
    qx_block = 3 * MOBA_WIDTH // XATTN_WIDTH
    return pl.pallas_call(
        _xattn_kernel,
        grid=(bsz, nt),
        in_specs=[
            pl.BlockSpec((XATTN_TQ, XATTN_WIDTH), lambda b, j: (b * nt + j, qx_block)),
            pl.BlockSpec((1, m, XATTN_WIDTH), lambda b, j: (b, 0, 0)),
            pl.BlockSpec((1, XATTN_HEADS * XATTN_VT_ROWS, m), lambda b, j: (b, 0, 0)),
        ],
        out_specs=pl.BlockSpec((XATTN_TQ, XATTN_WIDTH), lambda b, j: (b * nt + j, 0)),
        out_shape=jax.ShapeDtypeStruct((t, XATTN_WIDTH), BF16),
        name="xattn",
        compiler_params=_params("arbitrary", "arbitrary"),
    )(qkv, mk, mvt)


MERGE_TM = 1024


def _merge_kernel(x_ref, g_ref, ya_ref, yb_ref, yc_ref, wg_ref, wb_ref, wo_ref, o_ref):
    x = x_ref[...]
    h = _rms(x, g_ref[...]).astype(BF16)
    merged = None
    for n, y_ref in enumerate((ya_ref, yb_ref, yc_ref)):
        logits = jnp.dot(h, wg_ref[:, n * D_MODEL:(n + 1) * D_MODEL], preferred_element_type=F32)
        branch = jnp.dot(y_ref[...], wb_ref[n], preferred_element_type=F32)
        term = jax.nn.sigmoid(logits) * branch
        merged = term if merged is None else merged + term
    o_ref[...] = x + jnp.dot(merged.astype(BF16), wo_ref[...], preferred_element_type=F32)


def _merge(x2, gain, ya, yb, yc, w_in, w_branch, w_out, layer):
    t, d = x2.shape
    w = LRU_WIDTH
    return pl.pallas_call(
        _merge_kernel,
        grid=(t // MERGE_TM,),
        in_specs=[
            pl.BlockSpec((MERGE_TM, d), lambda i: (i, 0)),
            pl.BlockSpec((1, d), lambda i: (0, 0)),
            pl.BlockSpec((MERGE_TM, w), lambda i: (i, 0)),
            pl.BlockSpec((MERGE_TM, w), lambda i: (i, 0)),
            pl.BlockSpec((MERGE_TM, w), lambda i: (i, 0)),
            pl.BlockSpec((None, d, N_BRANCH * d), lambda i: (layer, 0, PROJ_COLS // (N_BRANCH * d))),
            pl.BlockSpec((None, N_BRANCH, w, d), lambda i: (layer, 0, 0, 0)),
            pl.BlockSpec((None, d, d), lambda i: (layer, 0, 0)),
        ],
        out_specs=pl.BlockSpec((MERGE_TM, d), lambda i: (i, 0)),
        out_shape=jax.ShapeDtypeStruct((t, d), F32),
        name="merge",
        compiler_params=_params("arbitrary"),
    )(x2, gain, ya, yb, yc, w_in, w_branch, w_out)


FFN_TM = 1024
FFN_GROUP = 1536
FFN_SUB = 768


def _ffn_kernel(x_ref, g_ref, wg_ref, wu_ref, cw_ref, cb_ref, wd_ref, fg_ref, o_ref, tail_ref,
                *, tiles_per_seq, final_norm):
    tm = FFN_TM
    i = pl.program_id(0)

    @pl.when(i == 0)
    def _():
        tail_ref[...] = jnp.zeros_like(tail_ref)

    x = x_ref[...]
    h = _rms(x, g_ref[...]).astype(BF16)
    sub = lax.broadcasted_iota(jnp.int32, (V7X_SUBLANES, FFN_SUB), 0)
    mid_seq = jnp.broadcast_to(i % tiles_per_seq, (V7X_SUBLANES, FFN_SUB)) != 0
    y = x
    for grp in range(D_FF // FFN_GROUP):
        acts = []
        for k in range(FFN_GROUP // FFN_SUB):
            lo = grp * FFN_GROUP + k * FFN_SUB
            g = jnp.dot(h, wg_ref[:, lo:lo + FFN_SUB], preferred_element_type=F32)
            u = jnp.dot(h, wu_ref[:, lo:lo + FFN_SUB], preferred_element_type=F32)
            prev = jnp.where(mid_seq, tail_ref[:, lo:lo + FFN_SUB], 0.0)
            tail_ref[:, lo:lo + FFN_SUB] = g[tm - V7X_SUBLANES:tm, :]
            conv = cb_ref[:, lo:lo + FFN_SUB] + cw_ref[FFN_CONV - 1:FFN_CONV, lo:lo + FFN_SUB] * g
            for shift in range(1, FFN_CONV):
                rolled = pltpu.roll(g, shift, 0)
                head = jnp.where(sub < shift, pltpu.roll(prev, shift, 0), rolled[0:V7X_SUBLANES, :])
                shifted = jnp.concatenate([head, rolled[V7X_SUBLANES:, :]], axis=0)
                tap = FFN_CONV - 1 - shift
                conv = conv + cw_ref[tap:tap + 1, lo:lo + FFN_SUB] * shifted
            acts.append((jax.nn.gelu(conv) * u).astype(BF16))
        act = jnp.concatenate(acts, axis=1)
        y = y + jnp.dot(act, wd_ref[grp * FFN_GROUP:(grp + 1) * FFN_GROUP, :], preferred_element_type=F32)
    if final_norm:
        y = _rms(y, fg_ref[...])
    o_ref[...] = y


def _ffn(x2, gain, w_gate, w_up, conv_w, conv_b, w_down, final_gain, layer, s, final_norm):
    t, d = x2.shape
    kern = functools.partial(_ffn_kernel, tiles_per_seq=s // FFN_TM, final_norm=final_norm)
    resident = lambda shape: pl.BlockSpec(shape, lambda i: (0,) * len(shape), pipeline_mode=pl.Buffered(1))
    stacked = lambda shape: pl.BlockSpec((None,) + shape, lambda i: (layer,) + (0,) * len(shape),
                                         pipeline_mode=pl.Buffered(1))
    return pl.pallas_call(
        kern,
        grid=(t // FFN_TM,),
        in_specs=[
            pl.BlockSpec((FFN_TM, d), lambda i: (i, 0)),
            resident((1, d)),
            stacked((d, D_FF)),
            stacked((d, D_FF)),
            resident((FFN_CONV, D_FF)),
            resident((1, D_FF)),
            stacked((D_FF, d)),
            resident((1, d)),
        ],
        out_specs=pl.BlockSpec((FFN_TM, d), lambda i: (i, 0)),
        out_shape=jax.ShapeDtypeStruct((t, d), F32),
        scratch_shapes=[pltpu.VMEM((V7X_SUBLANES, D_FF), F32)],
        name="ffn_final" if final_norm else "ffn",
        compiler_params=_params("arbitrary"),
    )(x2, gain, w_gate, w_up, conv_w, conv_b, w_down, final_gain)


def _block_diag(w):
    nh, hd, _ = w.shape
    eye = jnp.eye(nh, dtype=w.dtype)
    return (eye[:, None, :, None] * w[:, :, None, :]).reshape(nh * hd, nh * hd)


def kernel(x, mem, mix_norm_gain, w_in, lru_conv_w, lru_conv_b, lru_w_a, lru_b_a, lru_w_x, lru_b_x, lru_lambda,
           mem_norm_gain, w_mem_kv, w_branch, w_out, ffn_norm_gain, w_ffn_gate, w_ffn_up, ffn_conv_w, ffn_conv_b,
           w_ffn_down, final_norm_gain):
    bsz, s, d = x.shape
    depth = w_in.shape[0]
    x2 = x.reshape(bsz * s, d)
    slopes = jnp.exp2(-8.0 * jnp.arange(1, MOBA_HEADS + 1, dtype=F32) / MOBA_HEADS)
    row = lambda v: v.reshape(1, -1)
    assert PROJ_COLS % (N_BRANCH * d) == 0
    w_in_b, w_mem_b, w_branch_b, w_out_b = (t.astype(BF16) for t in (w_in, w_mem_kv, w_branch, w_out))
    w_gate_b, w_up_b, w_down_b = (t.astype(BF16) for t in (w_ffn_gate, w_ffn_up, w_ffn_down))
    for l in range(depth):
        w_gates = jnp.concatenate([_block_diag(lru_w_a[l]), _block_diag(lru_w_x[l])], axis=1).astype(BF16)
        b_gates = jnp.concatenate([lru_b_a[l].reshape(-1), lru_b_x[l].reshape(-1)]).reshape(1, -1)

        mk, mvt = _memkv(mem, row(mem_norm_gain[l]), w_mem_b, l)
        ya, qkv = _inproj_lru(x2, row(mix_norm_gain[l]), w_in_b, l, lru_conv_w[l], row(lru_conv_b[l]), w_gates, b_gates,
                              row(lru_lambda[l]), s)
        yb = _moba(slopes, qkv, bsz, s)
        yc = _xattn(qkv, mk, mvt, bsz, s)
        x2 = _merge(x2, row(mix_norm_gain[l]), ya, yb, yc, w_in_b, w_branch_b, w_out_b, l)
        x2 = _ffn(x2, row(ffn_norm_gain[l]), w_gate_b, w_up_b, ffn_conv_w[l], row(ffn_conv_b[l]), w_down_b,
                  row(final_norm_gain), l, s, final_norm=(l == depth - 1))
    return x2.reshape(bsz, s, d)
```
